```python
import math
import jax, jax.numpy as jnp
from jax import lax
import numpy as np

D_MODEL = 1024
BATCH = 4
SEQ = 8192
DEPTH = 2

CHUNK = 64
LEFT_CHUNKS = 8
BAND_CHUNKS = LEFT_CHUNKS + 1
BAND_LEN = BAND_CHUNKS * CHUNK
HEAD_DIM = 64
MIX_A_WIDTH = D_MODEL // 2
MIX_B_WIDTH = D_MODEL - MIX_A_WIDTH
A_HEADS = MIX_A_WIDTH // HEAD_DIM
MAX_REL = 2 * CHUNK
SSM_GROUP = 16
SSM_GROUPS = MIX_B_WIDTH // SSM_GROUP
SSM_STATE = 64
DT_MIN = 1e-3
DT_MAX = 1e-1
C_WIDTH = D_MODEL
C_HEADS = C_WIDTH // HEAD_DIM
SB_BLOCK = 128
D_FF = -(-8 * D_MODEL // (3 * 256)) * 256
DEEPNORM_ALPHA = (2 * DEPTH) ** 0.25
DEEPNORM_BETA = (8 * DEPTH) ** -0.25
LN_EPS = 1e-5
N_EVEN = (DEPTH + 1) // 2
N_ODD = DEPTH // 2
EVEN_IN = 3 * MIX_A_WIDTH + MIX_B_WIDTH
ODD_IN = 3 * C_WIDTH

kernel_name = "hybrid_chunkattn_s5_stickbreak_deepnorm"


def layer_norm(x, g, b):
    xf = x.astype(jnp.float32)
    mu = jnp.mean(xf, axis=-1, keepdims=True)
    var = jnp.mean(jnp.square(xf - mu), axis=-1, keepdims=True)
    y = (xf - mu) * lax.rsqrt(var + LN_EPS) * g.astype(jnp.float32) + b.astype(jnp.float32)
    return y.astype(x.dtype)


def chunk_band_attention(q, k, v, rel_table):
    b_, s_, _ = q.shape
    n_chunks = s_ // CHUNK

    def heads(t):
        return t.reshape(b_, n_chunks, CHUNK, A_HEADS, HEAD_DIM).transpose(0, 3, 1, 2, 4)

    band = jnp.arange(n_chunks)[:, None] + jnp.arange(BAND_CHUNKS)[None, :]
    valid = jnp.repeat(band >= LEFT_CHUNKS, CHUNK, axis=1)
    q_pos = LEFT_CHUNKS * CHUNK + jnp.arange(CHUNK)
    k_pos = jnp.arange(BAND_LEN)
    rel = jnp.clip(q_pos[:, None] - k_pos[None, :], -MAX_REL, MAX_REL) + MAX_REL
    bias = rel_table[:, rel].astype(jnp.float32)
    scale = HEAD_DIM ** -0.5
    pad = ((0, 0), (LEFT_CHUNKS, 0), (0, 0), (0, 0))

    def one_sequence(args):
        qh, kh, vh = args
        kb = jnp.pad(kh, pad)[:, band].reshape(A_HEADS, n_chunks, BAND_LEN, HEAD_DIM)
        vb = jnp.pad(vh, pad)[:, band].reshape(A_HEADS, n_chunks, BAND_LEN, HEAD_DIM)
        s = jnp.einsum('hcqd,hckd->hcqk', qh, kb).astype(jnp.float32) * scale + bias[:, None]
        s = jnp.where(valid[None, :, None, :], s, -1e30)
        p = jax.nn.softmax(s, axis=-1)
        return jnp.einsum('hcqk,hckd->hcqd', p.astype(vb.dtype), vb)

    o = lax.map(one_sequence, (heads(q), heads(k), heads(v)))
    return o.transpose(0, 2, 3, 1, 4).reshape(b_, s_, MIX_A_WIDTH)


def _complex_affine_combine(e1, e2):
    a1r, a1i, b1r, b1i = e1
    a2r, a2i, b2r, b2i = e2
    ar = a2r * a1r - a2i * a1i
    ai = a2r * a1i + a2i * a1r
    br = a2r * b1r - a2i * b1i + b2r
    bi = a2r * b1i + a2i * b1r + b2i
    return (ar, ai, br, bi)


def s5_mixer(u, lam_re, lam_im, b_re, b_im, c_re, c_im, d_skip, log_dt, glu_w, glu_b):
    b_, s_, _ = u.shape
    f32 = jnp.float32
    ug = u.reshape(b_, s_, SSM_GROUPS, SSM_GROUP).astype(f32)
    dt = jnp.exp(log_dt.astype(f32))[:, None]
    lr = lam_re.astype(f32)
    li = lam_im.astype(f32)
    mag = jnp.exp(lr * dt)
    ar = mag * jnp.cos(li * dt)
    ai = mag * jnp.sin(li * dt)
    den = lr * lr + li * li
    fr = ((ar - 1.0) * lr + ai * li) / den
    fi = (ai * lr - (ar - 1.0) * li) / den
    br_, bi_ = b_re.astype(f32), b_im.astype(f32)
    bbar_r = fr[..., None] * br_ - fi[..., None] * bi_
    bbar_i = fr[..., None] * bi_ + fi[..., None] * br_
    bu_r = jnp.einsum('bsgi,gpi->sbgp', ug, bbar_r)
    bu_i = jnp.einsum('bsgi,gpi->sbgp', ug, bbar_i)
    a_r = jnp.broadcast_to(ar[None, None], (s_, 1, SSM_GROUPS, SSM_STATE))
    a_i = jnp.broadcast_to(ai[None, None], (s_, 1, SSM_GROUPS, SSM_STATE))
    _, _, x_r, x_i = lax.associative_scan(_complex_affine_combine, (a_r, a_i, bu_r, bu_i), axis=0)
    y = (jnp.einsum('sbgp,gip->bsgi', x_r, c_re.astype(f32))
         - jnp.einsum('sbgp,gip->bsgi', x_i, c_im.astype(f32))
         + d_skip.astype(f32) * ug)
    y = jax.nn.gelu(y).reshape(b_, s_, MIX_B_WIDTH)
    y = y * jax.nn.sigmoid(y @ glu_w.astype(f32) + glu_b.astype(f32))
    return y.astype(u.dtype)


def stick_breaking_attention(q, k, v):
    b_, s_, _ = q.shape

    def heads(t):
        return t.reshape(b_, s_, C_HEADS, HEAD_DIM).transpose(0, 2, 1, 3)

    qh, kh, vh = heads(q), heads(k), heads(v)
    scale = HEAD_DIM ** -0.5
    outs = []
    for blk in range(s_ // SB_BLOCK):
        t0 = blk * SB_BLOCK
        end = t0 + SB_BLOCK
        qb = qh[:, :, t0:end]
        kp = kh[:, :, :end]
        vp = vh[:, :, :end]
        z = jnp.einsum('bhqd,bhkd->bhqk', qb, kp).astype(jnp.float32) * scale
        t_idx = t0 + jnp.arange(SB_BLOCK)
        s_idx = jnp.arange(end)
        strict = s_idx[None, :] < t_idx[:, None]
        log_keep = jnp.where(strict, jax.nn.log_sigmoid(-z), 0.0)
        rev = lax.cumsum(log_keep, axis=3, reverse=True)
        after = jnp.concatenate([rev[..., 1:], jnp.zeros_like(rev[..., :1])], axis=-1)
        w = jnp.where(strict, jnp.exp(jax.nn.log_sigmoid(z) + after), 0.0)
        outs.append(jnp.einsum('bhqk,bhkd->bhqd', w.astype(vp.dtype), vp))
    o = jnp.concatenate(outs, axis=2)
    return o.transpose(0, 2, 1, 3).reshape(b_, s_, C_WIDTH)


def swiglu(x, w1, w3, w2):
    return (jax.nn.silu(x @ w1) * (x @ w3)) @ w2


def setup_inputs(seed: int = 0) -> dict:
    key = jax.random.key(seed)
    ks = jax.random.split(key, 24)
    f32 = jnp.float32
    nrm = lambda k, shape, s: jax.random.normal(k, shape, f32) * s
    x = jax.random.normal(ks[0], (BATCH, SEQ, D_MODEL), f32)
    even_w_in = nrm(ks[1], (N_EVEN, D_MODEL, EVEN_IN), D_MODEL ** -0.5)
    even_rel_bias = nrm(ks[2], (N_EVEN, A_HEADS, 2 * MAX_REL + 1), 0.1)
    n_idx = jnp.arange(SSM_STATE, dtype=f32)
    ssm_lambda_re = -0.5 + nrm(ks[3], (N_EVEN, SSM_GROUPS, SSM_STATE), 0.01)
    ssm_lambda_im = jnp.broadcast_to(math.pi * n_idx, (N_EVEN, SSM_GROUPS, SSM_STATE)).astype(f32)
    b_scale = (2 * SSM_GROUP) ** -0.5
    ssm_b_re = nrm(ks[4], (N_EVEN, SSM_GROUPS, SSM_STATE, SSM_GROUP), b_scale)
    ssm_b_im = nrm(ks[5], (N_EVEN, SSM_GROUPS, SSM_STATE, SSM_GROUP), b_scale)
    c_scale = (2 * SSM_STATE) ** -0.5
    ssm_c_re = nrm(ks[6], (N_EVEN, SSM_GROUPS, SSM_GROUP, SSM_STATE), c_scale)
    ssm_c_im = nrm(ks[7], (N_EVEN, SSM_GROUPS, SSM_GROUP, SSM_STATE), c_scale)
    ssm_d = nrm(ks[8], (N_EVEN, SSM_GROUPS, SSM_GROUP), 1.0)
    ssm_log_dt = jax.random.uniform(ks[9], (N_EVEN, SSM_GROUPS), f32, math.log(DT_MIN), math.log(DT_MAX))
    ssm_glu_w = nrm(ks[10], (N_EVEN, MIX_B_WIDTH, MIX_B_WIDTH), MIX_B_WIDTH ** -0.5)
    ssm_glu_b = nrm(ks[11], (N_EVEN, MIX_B_WIDTH), 0.01)
    even_w_out = nrm(ks[12], (N_EVEN, D_MODEL, D_MODEL), DEEPNORM_BETA * D_MODEL ** -0.5)
    odd_w_in = nrm(ks[13], (N_ODD, D_MODEL, ODD_IN), D_MODEL ** -0.5)
    odd_w_out = nrm(ks[14], (N_ODD, C_WIDTH, D_MODEL), DEEPNORM_BETA * C_WIDTH ** -0.5)
    ffn_w1 = nrm(ks[15], (DEPTH, D_MODEL, D_FF), D_MODEL ** -0.5)
    ffn_w3 = nrm(ks[16], (DEPTH, D_MODEL, D_FF), D_MODEL ** -0.5)
    ffn_w2 = nrm(ks[17], (DEPTH, D_FF, D_MODEL), DEEPNORM_BETA * D_FF ** -0.5)
    ln_mix_g = 1.0 + nrm(ks[18], (DEPTH, D_MODEL), 0.01)
    ln_mix_b = nrm(ks[19], (DEPTH, D_MODEL), 0.01)
    ln_ffn_g = 1.0 + nrm(ks[20], (DEPTH, D_MODEL), 0.01)
    ln_ffn_b = nrm(ks[21], (DEPTH, D_MODEL), 0.01)
    return {"x": x, "even_w_in": even_w_in, "even_rel_bias": even_rel_bias,
            "ssm_lambda_re": ssm_lambda_re, "ssm_lambda_im": ssm_lambda_im,
            "ssm_b_re": ssm_b_re, "ssm_b_im": ssm_b_im, "ssm_c_re": ssm_c_re, "ssm_c_im": ssm_c_im,
            "ssm_d": ssm_d, "ssm_log_dt": ssm_log_dt, "ssm_glu_w": ssm_glu_w, "ssm_glu_b": ssm_glu_b,
            "even_w_out": even_w_out, "odd_w_in": odd_w_in, "odd_w_out": odd_w_out,
            "ffn_w1": ffn_w1, "ffn_w3": ffn_w3, "ffn_w2": ffn_w2,
            "ln_mix_g": ln_mix_g, "ln_mix_b": ln_mix_b, "ln_ffn_g": ln_ffn_g, "ln_ffn_b": ln_ffn_b}


def reference(x, even_w_in, even_rel_bias, ssm_lambda_re, ssm_lambda_im, ssm_b_re, ssm_b_im,
              ssm_c_re, ssm_c_im, ssm_d, ssm_log_dt, ssm_glu_w, ssm_glu_b, even_w_out,
              odd_w_in, odd_w_out, ffn_w1, ffn_w3, ffn_w2, ln_mix_g, ln_mix_b, ln_ffn_g, ln_ffn_b):
    h = x
    for layer in range(DEPTH):
        if layer % 2 == 0:
            e = layer // 2
            proj = h @ even_w_in[e]
            qa, ka, va, ub = jnp.split(proj, [MIX_A_WIDTH, 2 * MIX_A_WIDTH, 3 * MIX_A_WIDTH], axis=-1)
            oa = chunk_band_attention(qa, ka, va, even_rel_bias[e])
            ob = s5_mixer(ub, ssm_lambda_re[e], ssm_lambda_im[e], ssm_b_re[e], ssm_b_im[e],
                          ssm_c_re[e], ssm_c_im[e], ssm_d[e], ssm_log_dt[e], ssm_glu_w[e], ssm_glu_b[e])
            mix = jnp.concatenate([oa, ob], axis=-1) @ even_w_out[e]
        else:
            o = layer // 2
            qc, kc, vc = jnp.split(h @ odd_w_in[o], 3, axis=-1)
            mix = stick_breaking_attention(qc, kc, vc) @ odd_w_out[o]
        h = layer_norm(DEEPNORM_ALPHA * h + mix, ln_mix_g[layer], ln_mix_b[layer])
        ffn = swiglu(h, ffn_w1[layer], ffn_w3[layer], ffn_w2[layer])
        h = layer_norm(DEEPNORM_ALPHA * h + ffn, ln_ffn_g[layer], ln_ffn_b[layer])
    return h
```

```python
import functools
import math

import jax
import jax.numpy as jnp
from jax import lax
from jax.experimental import pallas as pl
from jax.experimental.pallas import tpu as pltpu

D_MODEL = 1024
DEPTH = 2
CHUNK = 64
LEFT_CHUNKS = 8
HEAD_DIM = 64
MIX_A_WIDTH = 512
MIX_B_WIDTH = 512
A_HEADS = 8
MAX_REL = 128
SSM_GROUP = 16
SSM_GROUPS = 32
SSM_STATE = 64
C_HEADS = 16
DEEPNORM_ALPHA = (2 * DEPTH) ** 0.25
LN_EPS = 1e-5
ATTN_SCALE = HEAD_DIM ** -0.5
NEG_BIG = -1e30

LANES = 128
VMEM_LIMIT_BYTES = 56 * 1024 * 1024

MM_TM, MM_TN = 1024, 512
LN_TM = 512
FFN_TM, FFN_TF = 1024, 256
CA_QB = 256
CA_KB = 3 * CA_QB
SSM_T = 128
SSM_LC = 512
SB_QB = 256
SB_KB = 256

BF16 = jnp.bfloat16
F32 = jnp.float32


def _params(semantics):
    return pltpu.CompilerParams(dimension_semantics=semantics, vmem_limit_bytes=VMEM_LIMIT_BYTES)


def _layer_norm_rows(y, g, b):
    mu = jnp.mean(y, axis=-1, keepdims=True)
    yc = y - mu
    var = jnp.mean(yc * yc, axis=-1, keepdims=True)
    return yc * lax.rsqrt(var + LN_EPS) * g + b


def _matmul_kernel(x_ref, w_ref, o_ref):
    o_ref[...] = jnp.dot(x_ref[...].astype(BF16), w_ref[...],
                         preferred_element_type=F32).astype(o_ref.dtype)


def _matmul(x, w, out_dtype):
    m, k = x.shape
    n = w.shape[1]
    tm, tn = min(MM_TM, m), min(MM_TN, n)
    assert m % tm == 0 and n % tn == 0
    return pl.pallas_call(
        _matmul_kernel,
        grid=(m // tm, n // tn),
        in_specs=[pl.BlockSpec((tm, k), lambda i, j: (i, 0)),
                  pl.BlockSpec((k, tn), lambda i, j: (0, j))],
        out_specs=pl.BlockSpec((tm, tn), lambda i, j: (i, j)),
        out_shape=jax.ShapeDtypeStruct((m, n), out_dtype),
        compiler_params=_params(("parallel", "arbitrary")),
        name="proj_matmul",
    )(x, w)


def _proj_ln_kernel(*refs, n_in):
    a_refs = refs[:n_in]
    w_refs = refs[n_in:2 * n_in]
    h_ref, g_ref, b_ref, o_ref = refs[2 * n_in:]
    mix = jnp.dot(a_refs[0][...], w_refs[0][...], preferred_element_type=F32)
    for a_ref, w_ref in zip(a_refs[1:], w_refs[1:]):
        mix += jnp.dot(a_ref[...], w_ref[...], preferred_element_type=F32)
    y = DEEPNORM_ALPHA * h_ref[...] + mix
    o_ref[...] = _layer_norm_rows(y, g_ref[...], b_ref[...])


def _proj_ln(acts, weights, h, g, b):
    m, d = h.shape
    tm = min(LN_TM, m)
    assert m % tm == 0
    n_in = len(acts)
    in_specs = ([pl.BlockSpec((tm, a.shape[1]), lambda i: (i, 0)) for a in acts]
                + [pl.BlockSpec(w.shape, lambda i: (0, 0)) for w in weights]
                + [pl.BlockSpec((tm, d), lambda i: (i, 0)),
                   pl.BlockSpec((1, d), lambda i: (0, 0)),
                   pl.BlockSpec((1, d), lambda i: (0, 0))])
    return pl.pallas_call(
        functools.partial(_proj_ln_kernel, n_in=n_in),
        grid=(m // tm,),
        in_specs=in_specs,
        out_specs=pl.BlockSpec((tm, d), lambda i: (i, 0)),
        out_shape=jax.ShapeDtypeStruct((m, d), F32),
        compiler_params=_params(("parallel",)),
        name="proj_ln",
    )(*acts, *weights, h, g, b)


def _ffn_kernel(x_ref, w1_ref, w3_ref, w2_ref, g_ref, b_ref, o_ref, xb_ref, acc_ref):
    f = pl.program_id(1)

    @pl.when(f == 0)
    def _():
        xb_ref[...] = x_ref[...].astype(BF16)
        acc_ref[...] = jnp.zeros_like(acc_ref)

    xb = xb_ref[...]
    h1 = jnp.dot(xb, w1_ref[...], preferred_element_type=F32)
    h3 = jnp.dot(xb, w3_ref[...], preferred_element_type=F32)
    gated = (h1 * jax.nn.sigmoid(h1) * h3).astype(BF16)
    acc_ref[...] += jnp.dot(gated, w2_ref[...], preferred_element_type=F32)

    @pl.when(f == pl.num_programs(1) - 1)
    def _():
        y = DEEPNORM_ALPHA * x_ref[...] + acc_ref[...]
        o_ref[...] = _layer_norm_rows(y, g_ref[...], b_ref[...])


def _ffn_ln(x, w1, w3, w2, g, b):
    m, d = x.shape
    dff = w1.shape[1]
    tm, tf = min(FFN_TM, m), min(FFN_TF, dff)
    assert m % tm == 0 and dff % tf == 0
    return pl.pallas_call(
        _ffn_kernel,
        grid=(m // tm, dff // tf),
        in_specs=[pl.BlockSpec((tm, d), lambda i, f: (i, 0)),
                  pl.BlockSpec((d, tf), lambda i, f: (0, f)),
                  pl.BlockSpec((d, tf), lambda i, f: (0, f)),
                  pl.BlockSpec((tf, d), lambda i, f: (f, 0)),
                  pl.BlockSpec((1, d), lambda i, f: (0, 0)),
                  pl.BlockSpec((1, d), lambda i, f: (0, 0))],
        out_specs=pl.BlockSpec((tm, d), lambda i, f: (i, 0)),
        out_shape=jax.ShapeDtypeStruct((m, d), F32),
        scratch_shapes=[pltpu.VMEM((tm, d), BF16), pltpu.VMEM((tm, d), F32)],
        compiler_params=_params(("parallel", "arbitrary")),
        name="ffn_ln",
    )(x, w1, w3, w2, g, b)


def _chunk_attn_kernel(q_ref, k0_ref, k1_ref, k2_ref, v0_ref, v1_ref, v2_ref, bias_ref, o_ref):
    qb = pl.program_id(2)
    q = q_ref[0]
    kk = jnp.concatenate([k0_ref[0], k1_ref[0], k2_ref[0]], axis=0)
    vv = jnp.concatenate([v0_ref[0], v1_ref[0], v2_ref[0]], axis=0)
    lane = lax.broadcasted_iota(jnp.int32, q.shape, 1)
    col = lax.broadcasted_iota(jnp.int32, (q.shape[0], kk.shape[0]), 1)
    col_valid = col >= (2 - qb) * q.shape[0]
    outs = []
    for hh in range(2):
        in_head = (lane >= hh * HEAD_DIM) & (lane < (hh + 1) * HEAD_DIM)
        qm = jnp.where(in_head, q, jnp.zeros_like(q))
        s = lax.dot_general(qm, kk, (((1,), (1,)), ((), ())), preferred_element_type=F32)
        s = s * ATTN_SCALE + bias_ref[hh]
        s = jnp.where(col_valid, s, NEG_BIG)
        m = jnp.max(s, axis=-1, keepdims=True)
        e = jnp.exp(s - m)
        l = jnp.sum(e, axis=-1, keepdims=True)
        pv = jnp.dot(e.astype(BF16), vv, preferred_element_type=F32)
        outs.append(pv / l)
    o_ref[0] = jnp.where(lane < HEAD_DIM, outs[0], outs[1]).astype(o_ref.dtype)


def _chunk_attention(qkv, bias):
    b_, s_, _ = qkv.shape
    n_pairs = MIX_A_WIDTH // LANES
    nq = s_ // CA_QB

    def kv_spec(col0, back):
        return pl.BlockSpec((1, CA_QB, LANES),
                            lambda b, p, i: (b, jnp.maximum(i - back, 0), col0 + p))

    return pl.pallas_call(
        _chunk_attn_kernel,
        grid=(b_, n_pairs, nq),
        in_specs=[pl.BlockSpec((1, CA_QB, LANES), lambda b, p, i: (b, i, p)),
                  kv_spec(n_pairs, 2), kv_spec(n_pairs, 1), kv_spec(n_pairs, 0),
                  kv_spec(2 * n_pairs, 2), kv_spec(2 * n_pairs, 1), kv_spec(2 * n_pairs, 0),
                  pl.BlockSpec((2, CA_QB, CA_KB), lambda b, p, i: (p, 0, 0))],
        out_specs=pl.BlockSpec((1, CA_QB, LANES), lambda b, p, i: (b, i, p)),
        out_shape=jax.ShapeDtypeStruct((b_, s_, MIX_A_WIDTH), BF16),
        compiler_params=_params(("parallel", "parallel", "arbitrary")),
        name="chunk_attn",
    )(qkv, qkv, qkv, qkv, qkv, qkv, qkv, bias)


def _band_bias(rel_table):
    cq = CA_QB // CHUNK
    qi = jnp.arange(CA_QB)
    ki = jnp.arange(CA_KB)
    dist = (qi[:, None] // CHUNK + LEFT_CHUNKS) - (ki[None, :] // CHUNK)
    in_band = (dist >= 0) & (dist <= LEFT_CHUNKS)
    rel = dist * CHUNK + (qi[:, None] % CHUNK) - (ki[None, :] % CHUNK)
    idx = jnp.clip(rel, -MAX_REL, MAX_REL) + MAX_REL
    assert CA_KB == (cq + LEFT_CHUNKS) * CHUNK
    bias = rel_table.astype(F32)[:, idx]
    return jnp.where(in_band[None], bias, NEG_BIG)


def _ssm_kernel(u_ref, bblk_ref, ar_ref, ai_ref, cblk_ref, d_ref, gw_ref, gb_ref, o_ref,
                bu_ref, st_ref):
    nb, tt, width = u_ref.shape
    n_state = ar_ref.shape[1]
    n_tiles = n_state // LANES
    tiles_per_ch = n_tiles // (width // LANES)

    @pl.when(pl.program_id(0) == 0)
    def _():
        st_ref[...] = jnp.zeros_like(st_ref)

    u = u_ref[...].reshape(nb * tt, width)
    ub = u.astype(BF16)
    for half in range(2):
        for k in range(0, n_tiles, 2):
            ch = (k // tiles_per_ch) * LANES
            col = half * n_state + k * LANES
            res = jnp.dot(ub[:, ch:ch + LANES], bblk_ref[ch:ch + LANES, col:col + 2 * LANES],
                          preferred_element_type=F32)
            bu_ref[half * n_tiles + k] = res[:, :LANES]
            bu_ref[half * n_tiles + k + 1] = res[:, LANES:]

    group = SSM_LC // LANES
    for c in range(0, n_tiles, group):
        ar = [jnp.broadcast_to(ar_ref[:, pl.ds((c + k) * LANES, LANES)], (nb, LANES))
              for k in range(group)]
        ai = [jnp.broadcast_to(ai_ref[:, pl.ds((c + k) * LANES, LANES)], (nb, LANES))
              for k in range(group)]

        def step(t, carry, c=c, ar=ar, ai=ai):
            rows = pl.ds(t, nb, stride=tt)
            out = []
            for k in range(group):
                xr, xi = carry[2 * k], carry[2 * k + 1]
                nxr = ar[k] * xr - ai[k] * xi + bu_ref[c + k, rows, :]
                nxi = ar[k] * xi + ai[k] * xr + bu_ref[n_tiles + c + k, rows, :]
                bu_ref[c + k, rows, :] = nxr
                bu_ref[n_tiles + c + k, rows, :] = nxi
                out += [nxr, nxi]
            return tuple(out)

        init = []
        for k in range(group):
            init += [st_ref[c + k], st_ref[n_tiles + c + k]]
        fin = lax.fori_loop(0, tt, step, tuple(init), unroll=8)
        for k in range(group):
            st_ref[c + k] = fin[2 * k]
            st_ref[n_tiles + c + k] = fin[2 * k + 1]

    ys = []
    for m in range(width // LANES):
        t0 = m * tiles_per_ch
        xs = [bu_ref[half * n_tiles + t0 + k] for half in range(2) for k in range(tiles_per_ch)]
        rows = [pl.ds(half * n_state + t0 * LANES, tiles_per_ch * LANES) for half in range(2)]
        cs = jnp.concatenate([cblk_ref[rows[0], m * LANES:(m + 1) * LANES],
                              cblk_ref[rows[1], m * LANES:(m + 1) * LANES]], axis=0)
        ys.append(jnp.dot(jnp.concatenate(xs, axis=1).astype(BF16), cs,
                          preferred_element_type=F32))
    y = jnp.concatenate(ys, axis=1)
    y = jax.nn.gelu(y + d_ref[...] * u)
    gate = jax.nn.sigmoid(jnp.dot(y.astype(BF16), gw_ref[...], preferred_element_type=F32)
                          + gb_ref[...])
    o_ref[...] = (y * gate).reshape(nb, tt, width).astype(o_ref.dtype)


def _ssm_glu(u, bblk, ar, ai, cblk, d_skip, glu_w, glu_b):
    b_, s_, width = u.shape
    n_state = ar.shape[1]
    tt = min(SSM_T, s_)
    assert s_ % tt == 0 and n_state % SSM_LC == 0
    const = lambda shape: pl.BlockSpec(shape, lambda t: (0,) * len(shape))
    return pl.pallas_call(
        _ssm_kernel,
        grid=(s_ // tt,),
        in_specs=[pl.BlockSpec((b_, tt, width), lambda t: (0, t, 0)),
                  const(bblk.shape), const(ar.shape), const(ai.shape), const(cblk.shape),
                  const(d_skip.shape), const(glu_w.shape), const(glu_b.shape)],
        out_specs=pl.BlockSpec((b_, tt, width), lambda t: (0, t, 0)),
        out_shape=jax.ShapeDtypeStruct((b_, s_, width), BF16),
        scratch_shapes=[pltpu.VMEM((2 * n_state // LANES, b_ * tt, LANES), F32),
                        pltpu.VMEM((2 * n_state // LANES, b_, LANES), F32)],
        compiler_params=_params(("arbitrary",)),
        name="ssm_glu",
    )(u, bblk, ar, ai, cblk, d_skip, glu_w, glu_b)


def _ssm_tables(lam_re, lam_im, b_re, b_im, c_re, c_im, log_dt):
    dt = jnp.exp(log_dt.astype(F32))[:, None]
    lr, li = lam_re.astype(F32), lam_im.astype(F32)
    mag = jnp.exp(lr * dt)
    ar = mag * jnp.cos(li * dt)
    ai = mag * jnp.sin(li * dt)
    den = lr * lr + li * li
    fr = ((ar - 1.0) * lr + ai * li) / den
    fi = (ai * lr - (ar - 1.0) * li) / den
    br_, bi_ = b_re.astype(F32), b_im.astype(F32)
    bbar_r = fr[..., None] * br_ - fi[..., None] * bi_
    bbar_i = fr[..., None] * bi_ + fi[..., None] * br_
    eye = jnp.eye(SSM_GROUPS, dtype=F32)
    n_state = SSM_GROUPS * SSM_STATE

    def in_map(bb):
        return jnp.einsum('gpi,gh->gihp', bb, eye).reshape(MIX_B_WIDTH, n_state)

    def out_map(cc):
        return jnp.einsum('gip,gh->gphi', cc, eye).reshape(n_state, MIX_B_WIDTH)

    bblk = jnp.concatenate([in_map(bbar_r), in_map(bbar_i)], axis=1).astype(BF16)
    cblk = jnp.concatenate([out_map(c_re.astype(F32)), -out_map(c_im.astype(F32))],
                           axis=0).astype(BF16)
    return bblk, ar.reshape(1, n_state), ai.reshape(1, n_state), cblk


def _softplus(z):
    return jnp.maximum(z, 0.0) + jnp.log1p(jnp.exp(-jnp.abs(z)))


def _sb_attn_kernel(q_ref, k_ref, v_ref, tri_ref, o_ref, acc_ref, carry_ref):
    i = pl.program_id(2)
    q = q_ref[0]
    lane = lax.broadcasted_iota(jnp.int32, q.shape, 1)
    qs = q * jnp.asarray(ATTN_SCALE, q.dtype)
    q_heads = [jnp.where((lane >= hh * HEAD_DIM) & (lane < (hh + 1) * HEAD_DIM), qs,
                         jnp.zeros_like(qs)) for hh in range(2)]
    tri = tri_ref[...]
    acc_ref[...] = jnp.zeros_like(acc_ref)
    carry_ref[...] = jnp.zeros_like(carry_ref)

    def key_block(j, diagonal):
        start = pl.multiple_of(j * SB_KB, SB_KB)
        kblk = k_ref[0, pl.ds(start, SB_KB), :]
        vblk = v_ref[0, pl.ds(start, SB_KB), :]
        if diagonal:
            row = lax.broadcasted_iota(jnp.int32, (SB_QB, SB_KB), 0)
            colk = lax.broadcasted_iota(jnp.int32, (SB_QB, SB_KB), 1)
            strict = colk < row
        for hh in range(2):
            z = lax.dot_general(q_heads[hh], kblk, (((1,), (1,)), ((), ())),
                                preferred_element_type=F32)
            sp = _softplus(z)
            if diagonal:
                sp = jnp.where(strict, sp, 0.0)
            hi = sp.astype(BF16)
            lo = (sp - hi.astype(F32)).astype(BF16)
            c = (jnp.dot(hi, tri, preferred_element_type=F32)
                 + jnp.dot(lo, tri, preferred_element_type=F32))
            w = jnp.exp(z + c)
            if diagonal:
                w = jnp.where(strict, w, 0.0)
            pv = jnp.dot(w.astype(BF16), vblk, preferred_element_type=F32)
            carry = carry_ref[hh]
            acc_ref[hh] += jnp.exp(carry) * pv
            carry_ref[hh] = carry + jnp.broadcast_to(c[:, 0:1], carry.shape)

    key_block(i, True)

    def body(jj, _):
        key_block(i - 1 - jj, False)
        return 0

    lax.fori_loop(0, i, body, 0)
    o_ref[0] = jnp.where(lane < HEAD_DIM, acc_ref[0], acc_ref[1]).astype(o_ref.dtype)


def _stick_breaking_attention(qkv, tri):
    b_, s_, width = qkv.shape
    n_pairs = width // 3 // LANES
    assert SB_QB == SB_KB and s_ % SB_QB == 0
    return pl.pallas_call(
        _sb_attn_kernel,
        grid=(b_, n_pairs, s_ // SB_QB),
        in_specs=[pl.BlockSpec((1, SB_QB, LANES), lambda b, p, i: (b, i, p)),
                  pl.BlockSpec((1, s_, LANES), lambda b, p, i: (b, 0, n_pairs + p)),
                  pl.BlockSpec((1, s_, LANES), lambda b, p, i: (b, 0, 2 * n_pairs + p)),
                  pl.BlockSpec((SB_KB, SB_KB), lambda b, p, i: (0, 0))],
        out_specs=pl.BlockSpec((1, SB_QB, LANES), lambda b, p, i: (b, i, p)),
        out_shape=jax.ShapeDtypeStruct((b_, s_, width // 3), BF16),
        scratch_shapes=[pltpu.VMEM((2, SB_QB, LANES), F32), pltpu.VMEM((2, SB_QB, LANES), F32)],
        compiler_params=_params(("parallel", "parallel", "arbitrary")),
        name="sb_attn",
    )(qkv, qkv, qkv, tri)


def kernel(x, even_w_in, even_rel_bias, ssm_lambda_re, ssm_lambda_im, ssm_b_re, ssm_b_im,
           ssm_c_re, ssm_c_im, ssm_d, ssm_log_dt, ssm_glu_w, ssm_glu_b, even_w_out,
           odd_w_in, odd_w_out, ffn_w1, ffn_w3, ffn_w2, ln_mix_g, ln_mix_b, ln_ffn_g, ln_ffn_b):
    b_, s_, d = x.shape
    m = b_ * s_
    row = lambda v: v.astype(F32).reshape(1, -1)
    h = x.reshape(m, d).astype(F32)
    for layer in range(DEPTH):
        if layer % 2 == 0:
            e = layer // 2
            w_in = even_w_in[e].astype(BF16)
            qkv = _matmul(h, w_in[:, :3 * MIX_A_WIDTH], BF16).reshape(b_, s_, 3 * MIX_A_WIDTH)
            u = _matmul(h, w_in[:, 3 * MIX_A_WIDTH:], F32).reshape(b_, s_, MIX_B_WIDTH)
            oa = _chunk_attention(qkv, _band_bias(even_rel_bias[e]))
            bblk, ar, ai, cblk = _ssm_tables(ssm_lambda_re[e], ssm_lambda_im[e], ssm_b_re[e],
                                             ssm_b_im[e], ssm_c_re[e], ssm_c_im[e], ssm_log_dt[e])
            ob = _ssm_glu(u, bblk, ar, ai, cblk, row(ssm_d[e]), ssm_glu_w[e].astype(BF16),
                          row(ssm_glu_b[e]))
            w_out = even_w_out[e].astype(BF16)
            acts = [oa.reshape(m, MIX_A_WIDTH), ob.reshape(m, MIX_B_WIDTH)]
            weights = [w_out[:MIX_A_WIDTH], w_out[MIX_A_WIDTH:]]
        else:
            o = layer // 2
            qkv = _matmul(h, odd_w_in[o].astype(BF16), BF16).reshape(b_, s_, 3 * d)
            ridx = jnp.arange(SB_KB)
            tri = jnp.where(ridx[:, None] >= ridx[None, :], -1.0, 0.0).astype(BF16)
            oc = _stick_breaking_attention(qkv, tri)
            acts = [oc.reshape(m, d)]
            weights = [odd_w_out[o].astype(BF16)]
        h = _proj_ln(acts, weights, h, row(ln_mix_g[layer]), row(ln_mix_b[layer]))
        h = _ffn_ln(h, ffn_w1[layer].astype(BF16), ffn_w3[layer].astype(BF16),
                    ffn_w2[layer].astype(BF16), row(ln_ffn_g[layer]), row(ln_ffn_b[layer]))
    return h.reshape(b_, s_, d).astype(x.dtype)
```

```python
import functools
import math

import jax
import jax.numpy as jnp
from jax import lax
from jax.experimental import pallas as pl
from jax.experimental.pallas import tpu as pltpu

D_MODEL = 1024
DEPTH = 2
CHUNK = 64
LEFT_CHUNKS = 8
HEAD_DIM = 64
MIX_A_WIDTH = 512
MIX_B_WIDTH = 512
A_HEADS = 8
MAX_REL = 128
SSM_GROUP = 16
SSM_GROUPS = 32
SSM_STATE = 64
C_HEADS = 16
DEEPNORM_ALPHA = (2 * DEPTH) ** 0.25
LN_EPS = 1e-5
ATTN_SCALE = HEAD_DIM ** -0.5
NEG_BIG = -1e30
LOG2E = 1.0 / math.log(2.0)

LANES = 128
VMEM_LIMIT_BYTES = 56 * 1024 * 1024

MM_TM, MM_TN = 512, 512
LN_TM = 512
FFN_TM, FFN_TF = 512, 256
CA_QB = 256
CA_KB = 3 * CA_QB
CA_LANES = 256
SSM_T = 128
SSM_LC = 512
SB_QB = 512
SB_KB = 256

BF16 = jnp.bfloat16
F32 = jnp.float32


def _params(semantics):
    return pltpu.CompilerParams(dimension_semantics=semantics, vmem_limit_bytes=VMEM_LIMIT_BYTES)


def _layer_norm_rows(y, g, b):
    mu = jnp.mean(y, axis=-1, keepdims=True)
    yc = y - mu
    var = jnp.mean(yc * yc, axis=-1, keepdims=True)
    return yc * lax.rsqrt(var + LN_EPS) * g + b


def _matmul_kernel(x_ref, w_ref, *o_refs):
    xb = x_ref[...].astype(BF16)
    col = 0
    for o_ref in o_refs:
        width = o_ref.shape[1]
        for c in range(0, width, MM_TN):
            o_ref[:, c:c + MM_TN] = jnp.dot(xb, w_ref[:, col + c:col + c + MM_TN],
                                            preferred_element_type=F32).astype(o_ref.dtype)
        col += width


def _matmul(x, w, outs):
    m, k = x.shape
    n = w.shape[1]
    tm = min(MM_TM, m)
    assert m % tm == 0 and sum(wd for wd, _ in outs) == n and all(wd % MM_TN == 0 for wd, _ in outs)
    return pl.pallas_call(
        _matmul_kernel,
        grid=(m // tm,),
        in_specs=[pl.BlockSpec((tm, k), lambda i: (i, 0)),
                  pl.BlockSpec((k, n), lambda i: (0, 0))],
        out_specs=[pl.BlockSpec((tm, wd), lambda i: (i, 0)) for wd, _ in outs],
        out_shape=[jax.ShapeDtypeStruct((m, wd), dt) for wd, dt in outs],
        compiler_params=_params(("parallel",)),
        name="proj_matmul",
    )(x, w)


def _proj_ln_kernel(*refs, n_in):
    a_refs = refs[:n_in]
    w_refs = refs[n_in:2 * n_in]
    h_ref, g_ref, b_ref, o_ref = refs[2 * n_in:]
    mix = jnp.dot(a_refs[0][...], w_refs[0][...], preferred_element_type=F32)
    for a_ref, w_ref in zip(a_refs[1:], w_refs[1:]):
        mix += jnp.dot(a_ref[...], w_ref[...], preferred_element_type=F32)
    y = DEEPNORM_ALPHA * h_ref[...] + mix
    o_ref[...] = _layer_norm_rows(y, g_ref[...], b_ref[...])


def _proj_ln(acts, weights, h, g, b):
    m, d = h.shape
    tm = min(LN_TM, m)
    assert m % tm == 0
    n_in = len(acts)
    in_specs = ([pl.BlockSpec((tm, a.shape[1]), lambda i: (i, 0)) for a in acts]
                + [pl.BlockSpec(w.shape, lambda i: (0, 0)) for w in weights]
                + [pl.BlockSpec((tm, d), lambda i: (i, 0)),
                   pl.BlockSpec((1, d), lambda i: (0, 0)),
                   pl.BlockSpec((1, d), lambda i: (0, 0))])
    return pl.pallas_call(
        functools.partial(_proj_ln_kernel, n_in=n_in),
        grid=(m // tm,),
        in_specs=in_specs,
        out_specs=pl.BlockSpec((tm, d), lambda i: (i, 0)),
        out_shape=jax.ShapeDtypeStruct((m, d), F32),
        compiler_params=_params(("parallel",)),
        name="proj_ln",
    )(*acts, *weights, h, g, b)


def _ffn_kernel(x_ref, w1_ref, w3_ref, w2_ref, g_ref, b_ref, o_ref):
    x = x_ref[...]
    xb = x.astype(BF16)
    acc = DEEPNORM_ALPHA * x
    for f in range(0, w1_ref.shape[1], FFN_TF):
        h1 = jnp.dot(xb, w1_ref[:, f:f + FFN_TF], preferred_element_type=F32)
        h3 = jnp.dot(xb, w3_ref[:, f:f + FFN_TF], preferred_element_type=F32)
        gated = (h1 * jax.nn.sigmoid(h1) * h3).astype(BF16)
        acc += jnp.dot(gated, w2_ref[f:f + FFN_TF, :], preferred_element_type=F32)
    o_ref[...] = _layer_norm_rows(acc, g_ref[...], b_ref[...])


def _ffn_ln(x, w1, w3, w2, g, b):
    m, d = x.shape
    dff = w1.shape[1]
    tm = min(FFN_TM, m)
    assert m % tm == 0 and dff % FFN_TF == 0
    const = lambda shape: pl.BlockSpec(shape, lambda i: (0, 0))
    return pl.pallas_call(
        _ffn_kernel,
        grid=(m // tm,),
        in_specs=[pl.BlockSpec((tm, d), lambda i: (i, 0)),
                  const((d, dff)), const((d, dff)), const((dff, d)), const((1, d)), const((1, d))],
        out_specs=pl.BlockSpec((tm, d), lambda i: (i, 0)),
        out_shape=jax.ShapeDtypeStruct((m, d), F32),
        compiler_params=_params(("parallel",)),
        name="ffn_ln",
    )(x, w1, w3, w2, g, b)


def _chunk_attn_kernel(q_ref, k0_ref, k1_ref, k2_ref, v0_ref, v1_ref, v2_ref, bias_ref, o_ref):
    qb = pl.program_id(2)
    q = q_ref[0]
    kk = jnp.concatenate([k0_ref[0], k1_ref[0], k2_ref[0]], axis=0)
    vv = jnp.concatenate([v0_ref[0], v1_ref[0], v2_ref[0]], axis=0)
    lane = lax.broadcasted_iota(jnp.int32, q.shape, 1)
    qs = q * jnp.asarray(ATTN_SCALE, q.dtype)
    col = lax.broadcasted_iota(jnp.int32, (q.shape[0], kk.shape[0]), 1)
    col_valid = col >= (2 - qb) * q.shape[0]
    out = jnp.zeros(q.shape, F32)
    for hh in range(q.shape[1] // HEAD_DIM):
        in_head = (lane >= hh * HEAD_DIM) & (lane < (hh + 1) * HEAD_DIM)
        qm = jnp.where(in_head, qs, jnp.zeros_like(qs))
        s = lax.dot_general(qm, kk, (((1,), (1,)), ((), ())), preferred_element_type=F32)
        s = jnp.where(col_valid, s + bias_ref[hh], NEG_BIG)
        m = jnp.max(s, axis=-1, keepdims=True)
        e = jnp.exp(s - m)
        l = jnp.sum(e, axis=-1, keepdims=True)
        pv = jnp.dot(e.astype(BF16), vv, preferred_element_type=F32)
        out = jnp.where(in_head, pv / l, out)
    o_ref[0] = out.astype(o_ref.dtype)


def _chunk_attention(qkv, bias):
    b_, s_, _ = qkv.shape
    n_grp = MIX_A_WIDTH // CA_LANES
    nq = s_ // CA_QB

    def kv_spec(col0, back):
        return pl.BlockSpec((1, CA_QB, CA_LANES),
                            lambda b, p, i: (b, jnp.maximum(i - back, 0), col0 + p))

    return pl.pallas_call(
        _chunk_attn_kernel,
        grid=(b_, n_grp, nq),
        in_specs=[pl.BlockSpec((1, CA_QB, CA_LANES), lambda b, p, i: (b, i, p)),
                  kv_spec(n_grp, 2), kv_spec(n_grp, 1), kv_spec(n_grp, 0),
                  kv_spec(2 * n_grp, 2), kv_spec(2 * n_grp, 1), kv_spec(2 * n_grp, 0),
                  pl.BlockSpec((CA_LANES // HEAD_DIM, CA_QB, CA_KB), lambda b, p, i: (p, 0, 0))],
        out_specs=pl.BlockSpec((1, CA_QB, CA_LANES), lambda b, p, i: (b, i, p)),
        out_shape=jax.ShapeDtypeStruct((b_, s_, MIX_A_WIDTH), BF16),
        compiler_params=_params(("parallel", "parallel", "arbitrary")),
        name="chunk_attn",
    )(qkv, qkv, qkv, qkv, qkv, qkv, qkv, bias)


def _band_bias(rel_table):
    assert CA_KB == CA_QB + LEFT_CHUNKS * CHUNK
    qi = jnp.arange(CA_QB)
    ki = jnp.arange(CA_KB)
    dist = (qi[:, None] // CHUNK + LEFT_CHUNKS) - (ki[None, :] // CHUNK)
    in_band = (dist >= 0) & (dist <= LEFT_CHUNKS)
    span = CA_QB + CA_KB - 1
    lo_pad = CA_KB - 1 - LEFT_CHUNKS * CHUNK - MAX_REL
    hi_pad = span - lo_pad - (2 * MAX_REL + 1)
    ext = jnp.pad(rel_table.astype(F32), ((0, 0), (lo_pad, hi_pad)), mode='edge')
    r = jnp.pad(ext[:, ::-1], ((0, 0), (0, 1)))
    shifted = jnp.tile(r, (1, CA_QB))[:, :CA_QB * span].reshape(-1, CA_QB, span)
    bias = shifted[:, :, CA_QB - 1:CA_QB - 1 + CA_KB]
    return jnp.where(in_band[None], bias, NEG_BIG)


def _ssm_kernel(u_ref, bblk_ref, ar_ref, ai_ref, cblk_ref, d_ref, gw_ref, gb_ref, o_ref,
                bu_ref, st_ref):
    nb, tt, width = u_ref.shape
    n_state = ar_ref.shape[1]
    n_tiles = n_state // LANES
    tiles_per_ch = n_tiles // (width // LANES)

    @pl.when(pl.program_id(0) == 0)
    def _():
        st_ref[...] = jnp.zeros_like(st_ref)

    u = u_ref[...].reshape(nb * tt, width)
    ub = u.astype(BF16)
    for half in range(2):
        for k in range(0, n_tiles, 2):
            ch = (k // tiles_per_ch) * LANES
            col = half * n_state + k * LANES
            res = jnp.dot(ub[:, ch:ch + LANES], bblk_ref[ch:ch + LANES, col:col + 2 * LANES],
                          preferred_element_type=F32)
            bu_ref[half * n_tiles + k] = res[:, :LANES]
            bu_ref[half * n_tiles + k + 1] = res[:, LANES:]

    group = SSM_LC // LANES
    for c in range(0, n_tiles, group):
        ar = [jnp.broadcast_to(ar_ref[:, pl.ds((c + k) * LANES, LANES)], (nb, LANES))
              for k in range(group)]
        ai = [jnp.broadcast_to(ai_ref[:, pl.ds((c + k) * LANES, LANES)], (nb, LANES))
              for k in range(group)]

        def step(t, carry, c=c, ar=ar, ai=ai):
            rows = pl.ds(t, nb, stride=tt)
            out = []
            for k in range(group):
                xr, xi = carry[2 * k], carry[2 * k + 1]
                nxr = ar[k] * xr - ai[k] * xi + bu_ref[c + k, rows, :]
                nxi = ar[k] * xi + ai[k] * xr + bu_ref[n_tiles + c + k, rows, :]
                bu_ref[c + k, rows, :] = nxr
                bu_ref[n_tiles + c + k, rows, :] = nxi
                out += [nxr, nxi]
            return tuple(out)

        init = []
        for k in range(group):
            init += [st_ref[c + k], st_ref[n_tiles + c + k]]
        fin = lax.fori_loop(0, tt, step, tuple(init), unroll=8)
        for k in range(group):
            st_ref[c + k] = fin[2 * k]
            st_ref[n_tiles + c + k] = fin[2 * k + 1]

    ys = []
    for m in range(width // LANES):
        t0 = m * tiles_per_ch
        xs = [bu_ref[half * n_tiles + t0 + k] for half in range(2) for k in range(tiles_per_ch)]
        rows = [pl.ds(half * n_state + t0 * LANES, tiles_per_ch * LANES) for half in range(2)]
        cs = jnp.concatenate([cblk_ref[rows[0], m * LANES:(m + 1) * LANES],
                              cblk_ref[rows[1], m * LANES:(m + 1) * LANES]], axis=0)
        ys.append(jnp.dot(jnp.concatenate(xs, axis=1).astype(BF16), cs,
                          preferred_element_type=F32))
    y = jnp.concatenate(ys, axis=1)
    y = jax.nn.gelu(y + d_ref[...] * u)
    gate = jax.nn.sigmoid(jnp.dot(y.astype(BF16), gw_ref[...], preferred_element_type=F32)
                          + gb_ref[...])
    o_ref[...] = (y * gate).reshape(nb, tt, width).astype(o_ref.dtype)


def _ssm_glu(u, bblk, ar, ai, cblk, d_skip, glu_w, glu_b):
    b_, s_, width = u.shape
    n_state = ar.shape[1]
    tt = min(SSM_T, s_)
    assert s_ % tt == 0 and n_state % SSM_LC == 0
    const = lambda shape: pl.BlockSpec(shape, lambda t: (0,) * len(shape))
    return pl.pallas_call(
        _ssm_kernel,
        grid=(s_ // tt,),
        in_specs=[pl.BlockSpec((b_, tt, width), lambda t: (0, t, 0)),
                  const(bblk.shape), const(ar.shape), const(ai.shape), const(cblk.shape),
                  const(d_skip.shape), const(glu_w.shape), const(glu_b.shape)],
        out_specs=pl.BlockSpec((b_, tt, width), lambda t: (0, t, 0)),
        out_shape=jax.ShapeDtypeStruct((b_, s_, width), BF16),
        scratch_shapes=[pltpu.VMEM((2 * n_state // LANES, b_ * tt, LANES), F32),
                        pltpu.VMEM((2 * n_state // LANES, b_, LANES), F32)],
        compiler_params=_params(("arbitrary",)),
        name="ssm_glu",
    )(u, bblk, ar, ai, cblk, d_skip, glu_w, glu_b)


def _ssm_tables(lam_re, lam_im, b_re, b_im, c_re, c_im, log_dt):
    dt = jnp.exp(log_dt.astype(F32))[:, None]
    lr, li = lam_re.astype(F32), lam_im.astype(F32)
    mag = jnp.exp(lr * dt)
    ar = mag * jnp.cos(li * dt)
    ai = mag * jnp.sin(li * dt)
    den = lr * lr + li * li
    fr = ((ar - 1.0) * lr + ai * li) / den
    fi = (ai * lr - (ar - 1.0) * li) / den
    br_, bi_ = b_re.astype(F32), b_im.astype(F32)
    bbar_r = fr[..., None] * br_ - fi[..., None] * bi_
    bbar_i = fr[..., None] * bi_ + fi[..., None] * br_
    eye = jnp.eye(SSM_GROUPS, dtype=F32)
    n_state = SSM_GROUPS * SSM_STATE

    def in_map(bb):
        return jnp.einsum('gpi,gh->gihp', bb, eye).reshape(MIX_B_WIDTH, n_state)

    def out_map(cc):
        return jnp.einsum('gip,gh->gphi', cc, eye).reshape(n_state, MIX_B_WIDTH)

    bblk = jnp.concatenate([in_map(bbar_r), in_map(bbar_i)], axis=1).astype(BF16)
    cblk = jnp.concatenate([out_map(c_re.astype(F32)), -out_map(c_im.astype(F32))],
                           axis=0).astype(BF16)
    return bblk, ar.reshape(1, n_state), ai.reshape(1, n_state), cblk


def _softplus(z):
    t = z * LOG2E
    neg_abs = lax.bitcast_convert_type(
        lax.bitcast_convert_type(t, jnp.uint32) | jnp.uint32(0x80000000), F32)
    return jnp.maximum(z, 0.0) + jnp.log(1.0 + jnp.exp2(neg_abs))


def _sb_attn_kernel(q_ref, k_ref, v_ref, tri_ref, o_ref, acc_ref, carry_ref):
    i = pl.program_id(2)
    q = q_ref[0]
    lane = lax.broadcasted_iota(jnp.int32, q.shape, 1)
    qs = q * jnp.asarray(ATTN_SCALE, q.dtype)
    zero = jnp.zeros_like(qs)
    qst = jnp.concatenate([jnp.where(lane < HEAD_DIM, qs, zero),
                           jnp.where(lane >= HEAD_DIM, qs, zero)], axis=0)
    acc_ref[...] = jnp.zeros_like(acc_ref)
    carry_ref[...] = jnp.zeros_like(carry_ref)

    def key_block(j, diagonal):
        start = pl.multiple_of(j * SB_KB, SB_KB)
        kblk = k_ref[0, pl.ds(start, SB_KB), :]
        vblk = v_ref[0, pl.ds(start, SB_KB), :]
        z = lax.dot_general(qst, kblk, (((1,), (1,)), ((), ())), preferred_element_type=F32)
        sp = _softplus(z)
        if diagonal:
            row = lax.broadcasted_iota(jnp.int32, z.shape, 0) & (SB_QB - 1)
            colk = lax.broadcasted_iota(jnp.int32, z.shape, 1)
            strict = colk + j * SB_KB < row + i * SB_QB
            sp = jnp.where(strict, sp, 0.0)
        c = jnp.dot(sp.astype(BF16), tri_ref[...], preferred_element_type=F32)
        w = jnp.exp(z + c)
        if diagonal:
            w = jnp.where(strict, w, 0.0)
        pv = jnp.dot(w.astype(BF16), vblk, preferred_element_type=F32)
        carry = carry_ref[...]
        acc_ref[...] += jnp.exp(carry) * pv
        carry_ref[...] = carry + jnp.broadcast_to(c[:, 0:1], carry.shape)

    ratio = SB_QB // SB_KB
    for r in reversed(range(ratio)):
        key_block(i * ratio + r, True)

    def body(jj, _):
        for r in range(ratio):
            key_block((i - jj) * ratio - 1 - r, False)
        return 0

    lax.fori_loop(0, i, body, 0)
    acc = acc_ref[...]
    o_ref[0] = jnp.where(lane < HEAD_DIM, acc[:SB_QB], acc[SB_QB:]).astype(o_ref.dtype)


def _stick_breaking_attention(qkv, tri):
    b_, s_, width = qkv.shape
    n_pairs = width // 3 // LANES
    assert SB_QB % SB_KB == 0 and SB_QB & (SB_QB - 1) == 0 and s_ % SB_QB == 0
    return pl.pallas_call(
        _sb_attn_kernel,
        grid=(b_, n_pairs, s_ // SB_QB),
        in_specs=[pl.BlockSpec((1, SB_QB, LANES), lambda b, p, i: (b, i, p)),
                  pl.BlockSpec((1, s_, LANES), lambda b, p, i: (b, 0, n_pairs + p)),
                  pl.BlockSpec((1, s_, LANES), lambda b, p, i: (b, 0, 2 * n_pairs + p)),
                  pl.BlockSpec((SB_KB, SB_KB), lambda b, p, i: (0, 0))],
        out_specs=pl.BlockSpec((1, SB_QB, LANES), lambda b, p, i: (b, i, p)),
        out_shape=jax.ShapeDtypeStruct((b_, s_, width // 3), BF16),
        scratch_shapes=[pltpu.VMEM((2 * SB_QB, LANES), F32), pltpu.VMEM((2 * SB_QB, LANES), F32)],
        compiler_params=_params(("parallel", "parallel", "arbitrary")),
        name="sb_attn",
    )(qkv, qkv, qkv, tri)


def kernel(x, even_w_in, even_rel_bias, ssm_lambda_re, ssm_lambda_im, ssm_b_re, ssm_b_im,
           ssm_c_re, ssm_c_im, ssm_d, ssm_log_dt, ssm_glu_w, ssm_glu_b, even_w_out,
           odd_w_in, odd_w_out, ffn_w1, ffn_w3, ffn_w2, ln_mix_g, ln_mix_b, ln_ffn_g, ln_ffn_b):
    b_, s_, d = x.shape
    m = b_ * s_
    row = lambda v: v.astype(F32).reshape(1, -1)
    h = x.reshape(m, d).astype(F32)
    for layer in range(DEPTH):
        if layer % 2 == 0:
            e = layer // 2
            qkv, u = _matmul(h, even_w_in[e].astype(BF16),
                             [(3 * MIX_A_WIDTH, BF16), (MIX_B_WIDTH, F32)])
            qkv = qkv.reshape(b_, s_, 3 * MIX_A_WIDTH)
            u = u.reshape(b_, s_, MIX_B_WIDTH)
            oa = _chunk_attention(qkv, _band_bias(even_rel_bias[e]))
            bblk, ar, ai, cblk = _ssm_tables(ssm_lambda_re[e], ssm_lambda_im[e], ssm_b_re[e],
                                             ssm_b_im[e], ssm_c_re[e], ssm_c_im[e], ssm_log_dt[e])
            ob = _ssm_glu(u, bblk, ar, ai, cblk, row(ssm_d[e]), ssm_glu_w[e].astype(BF16),
                          row(ssm_glu_b[e]))
            w_out = even_w_out[e].astype(BF16)
            acts = [oa.reshape(m, MIX_A_WIDTH), ob.reshape(m, MIX_B_WIDTH)]
            weights = [w_out[:MIX_A_WIDTH], w_out[MIX_A_WIDTH:]]
        else:
            o = layer // 2
            (qkv,) = _matmul(h, odd_w_in[o].astype(BF16), [(3 * d, BF16)])
            qkv = qkv.reshape(b_, s_, 3 * d)
            ridx = jnp.arange(SB_KB)
            tri = jnp.where(ridx[:, None] >= ridx[None, :], -1.0, 0.0).astype(BF16)
            oc = _stick_breaking_attention(qkv, tri)
            acts = [oc.reshape(m, d)]
            weights = [odd_w_out[o].astype(BF16)]
        h = _proj_ln(acts, weights, h, row(ln_mix_g[layer]), row(ln_mix_b[layer]))
        h = _ffn_ln(h, ffn_w1[layer].astype(BF16), ffn_w3[layer].astype(BF16),
                    ffn_w2[layer].astype(BF16), row(ln_ffn_g[layer]), row(ln_ffn_b[layer]))
    return h.reshape(b_, s_, d).astype(x.dtype)
```

```python
import functools
import math

import jax
import jax.numpy as jnp
from jax import lax
from jax.experimental import pallas as pl
from jax.experimental.pallas import tpu as pltpu

D_MODEL = 1024
DEPTH = 2
CHUNK = 64
LEFT_CHUNKS = 8
HEAD_DIM = 64
MIX_A_WIDTH = 512
MIX_B_WIDTH = 512
A_HEADS = 8
MAX_REL = 128
SSM_GROUP = 16
SSM_GROUPS = 32
SSM_STATE = 64
C_HEADS = 16
DEEPNORM_ALPHA = (2 * DEPTH) ** 0.25
LN_EPS = 1e-5
ATTN_SCALE = HEAD_DIM ** -0.5
NEG_BIG = -1e30
LOG2E = 1.0 / math.log(2.0)
EXP_ZERO_BELOW = -104.0

LANES = 128
SUBLANES = 8
VMEM_LIMIT_BYTES = 56 * 1024 * 1024

MM_TM, MM_TN = 512, 512
LN_TM = 512
FFN_TM, FFN_TF = 512, 256
CA_QB = 256
CA_KB = 3 * CA_QB
CA_LANES = 256
SSM_T = 128
SSM_LC = 512
SB_QB = 512
SB_KB = 256

BF16 = jnp.bfloat16
F32 = jnp.float32


def _params(semantics):
    return pltpu.CompilerParams(dimension_semantics=semantics, vmem_limit_bytes=VMEM_LIMIT_BYTES)


def _layer_norm_rows(y, g, b):
    mu = jnp.mean(y, axis=-1, keepdims=True)
    yc = y - mu
    var = jnp.mean(yc * yc, axis=-1, keepdims=True)
    return yc * lax.rsqrt(var + LN_EPS) * g + b


def _matmul_kernel(x_ref, w_ref, *o_refs):
    xb = x_ref[...].astype(BF16)
    col = 0
    for o_ref in o_refs:
        width = o_ref.shape[1]
        for c in range(0, width, MM_TN):
            o_ref[:, c:c + MM_TN] = jnp.dot(xb, w_ref[:, col + c:col + c + MM_TN],
                                            preferred_element_type=F32).astype(o_ref.dtype)
        col += width


def _matmul(x, w, outs):
    m, k = x.shape
    n = w.shape[1]
    tm = min(MM_TM, m)
    assert m % tm == 0 and sum(wd for wd, _ in outs) == n and all(wd % MM_TN == 0 for wd, _ in outs)
    return pl.pallas_call(
        _matmul_kernel,
        grid=(m // tm,),
        in_specs=[pl.BlockSpec((tm, k), lambda i: (i, 0)),
                  pl.BlockSpec((k, n), lambda i: (0, 0))],
        out_specs=[pl.BlockSpec((tm, wd), lambda i: (i, 0)) for wd, _ in outs],
        out_shape=[jax.ShapeDtypeStruct((m, wd), dt) for wd, dt in outs],
        compiler_params=_params(("parallel",)),
        name="proj_matmul",
    )(x, w)


def _proj_ln_kernel(*refs, n_in):
    a_refs = refs[:n_in]
    w_refs = refs[n_in:2 * n_in]
    h_ref, g_ref, b_ref, o_ref = refs[2 * n_in:]
    mix = jnp.dot(a_refs[0][...], w_refs[0][...], preferred_element_type=F32)
    for a_ref, w_ref in zip(a_refs[1:], w_refs[1:]):
        mix += jnp.dot(a_ref[...], w_ref[...], preferred_element_type=F32)
    y = DEEPNORM_ALPHA * h_ref[...] + mix
    o_ref[...] = _layer_norm_rows(y, g_ref[...], b_ref[...])


def _proj_ln(acts, weights, h, g, b):
    m, d = h.shape
    tm = min(LN_TM, m)
    assert m % tm == 0
    n_in = len(acts)
    in_specs = ([pl.BlockSpec((tm, a.shape[1]), lambda i: (i, 0)) for a in acts]
                + [pl.BlockSpec(w.shape, lambda i: (0, 0)) for w in weights]
                + [pl.BlockSpec((tm, d), lambda i: (i, 0)),
                   pl.BlockSpec((1, d), lambda i: (0, 0)),
                   pl.BlockSpec((1, d), lambda i: (0, 0))])
    return pl.pallas_call(
        functools.partial(_proj_ln_kernel, n_in=n_in),
        grid=(m // tm,),
        in_specs=in_specs,
        out_specs=pl.BlockSpec((tm, d), lambda i: (i, 0)),
        out_shape=jax.ShapeDtypeStruct((m, d), F32),
        compiler_params=_params(("parallel",)),
        name="proj_ln",
    )(*acts, *weights, h, g, b)


def _ffn_kernel(x_ref, w1_ref, w3_ref, w2_ref, g_ref, b_ref, o_ref):
    x = x_ref[...]
    xb = x.astype(BF16)
    acc = DEEPNORM_ALPHA * x
    for f in range(0, w1_ref.shape[1], FFN_TF):
        h1 = jnp.dot(xb, w1_ref[:, f:f + FFN_TF], preferred_element_type=F32)
        h3 = jnp.dot(xb, w3_ref[:, f:f + FFN_TF], preferred_element_type=F32)
        gated = (h1 * jax.nn.sigmoid(h1) * h3).astype(BF16)
        acc += jnp.dot(gated, w2_ref[f:f + FFN_TF, :], preferred_element_type=F32)
    o_ref[...] = _layer_norm_rows(acc, g_ref[...], b_ref[...])


def _ffn_ln(x, w1, w3, w2, g, b):
    m, d = x.shape
    dff = w1.shape[1]
    tm = min(FFN_TM, m)
    assert m % tm == 0 and dff % FFN_TF == 0
    const = lambda shape: pl.BlockSpec(shape, lambda i: (0, 0))
    return pl.pallas_call(
        _ffn_kernel,
        grid=(m // tm,),
        in_specs=[pl.BlockSpec((tm, d), lambda i: (i, 0)),
                  const((d, dff)), const((d, dff)), const((dff, d)), const((1, d)), const((1, d))],
        out_specs=pl.BlockSpec((tm, d), lambda i: (i, 0)),
        out_shape=jax.ShapeDtypeStruct((m, d), F32),
        compiler_params=_params(("parallel",)),
        name="ffn_ln",
    )(x, w1, w3, w2, g, b)


def _chunk_attn_kernel(q_ref, k0_ref, k1_ref, k2_ref, v0_ref, v1_ref, v2_ref, bias_ref, o_ref):
    qb = pl.program_id(2)
    q = q_ref[0]
    kk = jnp.concatenate([k0_ref[0], k1_ref[0], k2_ref[0]], axis=0)
    vv = jnp.concatenate([v0_ref[0], v1_ref[0], v2_ref[0]], axis=0)
    lane = lax.broadcasted_iota(jnp.int32, q.shape, 1)
    qs = q * jnp.asarray(ATTN_SCALE, q.dtype)
    col = lax.broadcasted_iota(jnp.int32, (q.shape[0], kk.shape[0]), 1)
    col_valid = col >= (2 - qb) * q.shape[0]
    out = jnp.zeros(q.shape, F32)
    for hh in range(q.shape[1] // HEAD_DIM):
        in_head = (lane >= hh * HEAD_DIM) & (lane < (hh + 1) * HEAD_DIM)
        qm = jnp.where(in_head, qs, jnp.zeros_like(qs))
        s = lax.dot_general(qm, kk, (((1,), (1,)), ((), ())), preferred_element_type=F32)
        s = jnp.where(col_valid, s + bias_ref[hh], NEG_BIG)
        m = jnp.max(s, axis=-1, keepdims=True)
        e = jnp.exp(s - m)
        l = jnp.sum(e, axis=-1, keepdims=True)
        pv = jnp.dot(e.astype(BF16), vv, preferred_element_type=F32)
        out = jnp.where(in_head, pv / l, out)
    o_ref[0] = out.astype(o_ref.dtype)


def _chunk_attention(qkv, bias):
    b_, s_, _ = qkv.shape
    n_grp = MIX_A_WIDTH // CA_LANES
    nq = s_ // CA_QB

    def kv_spec(col0, back):
        return pl.BlockSpec((1, CA_QB, CA_LANES),
                            lambda b, p, i: (b, jnp.maximum(i - back, 0), col0 + p))

    return pl.pallas_call(
        _chunk_attn_kernel,
        grid=(b_, n_grp, nq),
        in_specs=[pl.BlockSpec((1, CA_QB, CA_LANES), lambda b, p, i: (b, i, p)),
                  kv_spec(n_grp, 2), kv_spec(n_grp, 1), kv_spec(n_grp, 0),
                  kv_spec(2 * n_grp, 2), kv_spec(2 * n_grp, 1), kv_spec(2 * n_grp, 0),
                  pl.BlockSpec((CA_LANES // HEAD_DIM, CA_QB, CA_KB), lambda b, p, i: (p, 0, 0))],
        out_specs=pl.BlockSpec((1, CA_QB, CA_LANES), lambda b, p, i: (b, i, p)),
        out_shape=jax.ShapeDtypeStruct((b_, s_, MIX_A_WIDTH), BF16),
        compiler_params=_params(("parallel", "parallel", "arbitrary")),
        name="chunk_attn",
    )(qkv, qkv, qkv, qkv, qkv, qkv, qkv, bias)


def _band_bias(rel_table):
    assert CA_KB == CA_QB + LEFT_CHUNKS * CHUNK
    qi = jnp.arange(CA_QB)
    ki = jnp.arange(CA_KB)
    dist = (qi[:, None] // CHUNK + LEFT_CHUNKS) - (ki[None, :] // CHUNK)
    in_band = (dist >= 0) & (dist <= LEFT_CHUNKS)
    span = CA_QB + CA_KB - 1
    lo_pad = CA_KB - 1 - LEFT_CHUNKS * CHUNK - MAX_REL
    hi_pad = span - lo_pad - (2 * MAX_REL + 1)
    ext = jnp.pad(rel_table.astype(F32), ((0, 0), (lo_pad, hi_pad)), mode='edge')
    r = jnp.pad(ext[:, ::-1], ((0, 0), (0, 1)))
    shifted = jnp.tile(r, (1, CA_QB))[:, :CA_QB * span].reshape(-1, CA_QB, span)
    bias = shifted[:, :, CA_QB - 1:CA_QB - 1 + CA_KB]
    return jnp.where(in_band[None], bias, NEG_BIG)


def _ssm_kernel(u_ref, bblk_ref, ar_ref, ai_ref, cblk_ref, d_ref, gw_ref, gb_ref, o_ref,
                bu_ref, st_ref):
    n_rows, width = u_ref.shape
    n_state = ar_ref.shape[1]
    n_tiles = n_state // LANES
    tiles_per_ch = n_tiles // (width // LANES)

    @pl.when(pl.program_id(0) == 0)
    def _():
        st_ref[...] = jnp.zeros_like(st_ref)

    u = u_ref[...]
    ub = u.astype(BF16)
    for half in range(2):
        for k in range(0, n_tiles, 2):
            ch = (k // tiles_per_ch) * LANES
            col = half * n_state + k * LANES
            res = jnp.dot(ub[:, ch:ch + LANES], bblk_ref[ch:ch + LANES, col:col + 2 * LANES],
                          preferred_element_type=F32)
            bu_ref[half * n_tiles + k] = res[:, :LANES]
            bu_ref[half * n_tiles + k + 1] = res[:, LANES:]

    group = SSM_LC // LANES
    half = SUBLANES // 2
    first = lax.broadcasted_iota(jnp.int32, (SUBLANES, LANES), 0) < half
    for c in range(0, n_tiles, group):
        ar = [jnp.broadcast_to(ar_ref[:, pl.ds((c + k) * LANES, LANES)], (SUBLANES, LANES))
              for k in range(group)]
        ai = [jnp.broadcast_to(ai_ref[:, pl.ds((c + k) * LANES, LANES)], (SUBLANES, LANES))
              for k in range(group)]

        def two_steps(kt, carry, c=c, ar=ar, ai=ai):
            rows = pl.ds(pl.multiple_of(kt * SUBLANES, SUBLANES), SUBLANES)
            out = []
            for k in range(group):
                xr, xi = carry[2 * k], carry[2 * k + 1]
                br = bu_ref[c + k, rows, :]
                bi = bu_ref[n_tiles + c + k, rows, :]
                xr, xi = pltpu.roll(xr, half, 0), pltpu.roll(xi, half, 0)
                er = ar[k] * xr - ai[k] * xi + br
                ei = ar[k] * xi + ai[k] * xr + bi
                xr, xi = pltpu.roll(er, half, 0), pltpu.roll(ei, half, 0)
                orr = ar[k] * xr - ai[k] * xi + br
                oi = ar[k] * xi + ai[k] * xr + bi
                bu_ref[c + k, rows, :] = jnp.where(first, er, orr)
                bu_ref[n_tiles + c + k, rows, :] = jnp.where(first, ei, oi)
                out += [orr, oi]
            return tuple(out)

        init = []
        for k in range(group):
            init += [st_ref[c + k], st_ref[n_tiles + c + k]]
        fin = lax.fori_loop(0, n_rows // SUBLANES, two_steps, tuple(init), unroll=4)
        for k in range(group):
            st_ref[c + k] = fin[2 * k]
            st_ref[n_tiles + c + k] = fin[2 * k + 1]

    ys = []
    for m in range(width // LANES):
        t0 = m * tiles_per_ch
        xs = [bu_ref[half * n_tiles + t0 + k] for half in range(2) for k in range(tiles_per_ch)]
        rows = [pl.ds(half * n_state + t0 * LANES, tiles_per_ch * LANES) for half in range(2)]
        cs = jnp.concatenate([cblk_ref[rows[0], m * LANES:(m + 1) * LANES],
                              cblk_ref[rows[1], m * LANES:(m + 1) * LANES]], axis=0)
        ys.append(jnp.dot(jnp.concatenate(xs, axis=1).astype(BF16), cs,
                          preferred_element_type=F32))
    y = jnp.concatenate(ys, axis=1)
    y = jax.nn.gelu(y + d_ref[...] * u)
    gate = jax.nn.sigmoid(jnp.dot(y.astype(BF16), gw_ref[...], preferred_element_type=F32)
                          + gb_ref[...])
    o_ref[...] = (y * gate).astype(o_ref.dtype)


def _ssm_glu(u, bblk, ar, ai, cblk, d_skip, glu_w, glu_b):
    b_, s_, width = u.shape
    n_state = ar.shape[1]
    tt = min(SSM_T, s_)
    assert s_ % tt == 0 and n_state % SSM_LC == 0
    assert 2 * b_ == SUBLANES, "the scan packs two time steps of a batch of 4 into one 8-row tile"
    rows = tt * b_
    u_tm = u.transpose(1, 0, 2).reshape(s_ * b_, width)
    const = lambda shape: pl.BlockSpec(shape, lambda t: (0,) * len(shape))
    out = pl.pallas_call(
        _ssm_kernel,
        grid=(s_ // tt,),
        in_specs=[pl.BlockSpec((rows, width), lambda t: (t, 0)),
                  const(bblk.shape), const(ar.shape), const(ai.shape), const(cblk.shape),
                  const(d_skip.shape), const(glu_w.shape), const(glu_b.shape)],
        out_specs=pl.BlockSpec((rows, width), lambda t: (t, 0)),
        out_shape=jax.ShapeDtypeStruct((s_ * b_, width), BF16),
        scratch_shapes=[pltpu.VMEM((2 * n_state // LANES, rows, LANES), F32),
                        pltpu.VMEM((2 * n_state // LANES, SUBLANES, LANES), F32)],
        compiler_params=_params(("arbitrary",)),
        name="ssm_glu",
    )(u_tm, bblk, ar, ai, cblk, d_skip, glu_w, glu_b)
    return out.reshape(s_, b_, width).transpose(1, 0, 2)


def _ssm_tables(lam_re, lam_im, b_re, b_im, c_re, c_im, log_dt):
    dt = jnp.exp(log_dt.astype(F32))[:, None]
    lr, li = lam_re.astype(F32), lam_im.astype(F32)
    mag = jnp.exp(lr * dt)
    ar = mag * jnp.cos(li * dt)
    ai = mag * jnp.sin(li * dt)
    den = lr * lr + li * li
    fr = ((ar - 1.0) * lr + ai * li) / den
    fi = (ai * lr - (ar - 1.0) * li) / den
    br_, bi_ = b_re.astype(F32), b_im.astype(F32)
    bbar_r = fr[..., None] * br_ - fi[..., None] * bi_
    bbar_i = fr[..., None] * bi_ + fi[..., None] * br_
    eye = jnp.eye(SSM_GROUPS, dtype=F32)
    n_state = SSM_GROUPS * SSM_STATE

    def in_map(bb):
        return jnp.einsum('gpi,gh->gihp', bb, eye).reshape(MIX_B_WIDTH, n_state)

    def out_map(cc):
        return jnp.einsum('gip,gh->gphi', cc, eye).reshape(n_state, MIX_B_WIDTH)

    bblk = jnp.concatenate([in_map(bbar_r), in_map(bbar_i)], axis=1).astype(BF16)
    cblk = jnp.concatenate([out_map(c_re.astype(F32)), -out_map(c_im.astype(F32))],
                           axis=0).astype(BF16)
    return bblk, ar.reshape(1, n_state), ai.reshape(1, n_state), cblk


def _softplus(z):
    t = z * LOG2E
    neg_abs = lax.bitcast_convert_type(
        lax.bitcast_convert_type(t, jnp.uint32) | jnp.uint32(0x80000000), F32)
    return jnp.maximum(z, 0.0) + jnp.log(1.0 + jnp.exp2(neg_abs))


def _sb_attn_kernel(q_ref, k_ref, v_ref, tri_ref, o_ref, acc_ref, carry_ref):
    i = pl.program_id(2)
    q = q_ref[0]
    lane = lax.broadcasted_iota(jnp.int32, q.shape, 1)
    qs = q * jnp.asarray(ATTN_SCALE, q.dtype)
    zero = jnp.zeros_like(qs)
    qst = jnp.concatenate([jnp.where(lane < HEAD_DIM, qs, zero),
                           jnp.where(lane >= HEAD_DIM, qs, zero)], axis=0)
    acc_ref[...] = jnp.zeros_like(acc_ref)
    carry_ref[...] = jnp.zeros_like(carry_ref)

    def key_block(j, diagonal):
        start = pl.multiple_of(j * SB_KB, SB_KB)
        kblk = k_ref[0, pl.ds(start, SB_KB), :]
        vblk = v_ref[0, pl.ds(start, SB_KB), :]
        z = lax.dot_general(qst, kblk, (((1,), (1,)), ((), ())), preferred_element_type=F32)
        sp = _softplus(z)
        if diagonal:
            row = lax.broadcasted_iota(jnp.int32, z.shape, 0) & (SB_QB - 1)
            colk = lax.broadcasted_iota(jnp.int32, z.shape, 1)
            strict = colk + j * SB_KB < row + i * SB_QB
            sp = jnp.where(strict, sp, 0.0)
        c = jnp.dot(sp.astype(BF16), tri_ref[...], preferred_element_type=F32)
        w = jnp.exp(z + c)
        if diagonal:
            w = jnp.where(strict, w, 0.0)
        pv = jnp.dot(w.astype(BF16), vblk, preferred_element_type=F32)
        carry = carry_ref[...]
        acc_ref[...] += jnp.exp(carry) * pv
        carry_ref[...] = carry + jnp.broadcast_to(c[:, 0:1], carry.shape)

    ratio = SB_QB // SB_KB
    for r in reversed(range(ratio)):
        key_block(i * ratio + r, True)

    n_left = i * ratio

    def cond(state):
        jj, alive = state
        return (jj < n_left) & alive

    def body(state):
        jj, _ = state
        key_block(n_left - 1 - jj, False)
        return jj + 1, jnp.max(carry_ref[...]) >= EXP_ZERO_BELOW

    lax.while_loop(cond, body, (jnp.int32(0), True))
    acc = acc_ref[...]
    o_ref[0] = jnp.where(lane < HEAD_DIM, acc[:SB_QB], acc[SB_QB:]).astype(o_ref.dtype)


def _stick_breaking_attention(qkv, tri):
    b_, s_, width = qkv.shape
    n_pairs = width // 3 // LANES
    assert SB_QB % SB_KB == 0 and SB_QB & (SB_QB - 1) == 0 and s_ % SB_QB == 0
    return pl.pallas_call(
        _sb_attn_kernel,
        grid=(b_, n_pairs, s_ // SB_QB),
        in_specs=[pl.BlockSpec((1, SB_QB, LANES), lambda b, p, i: (b, i, p)),
                  pl.BlockSpec((1, s_, LANES), lambda b, p, i: (b, 0, n_pairs + p)),
                  pl.BlockSpec((1, s_, LANES), lambda b, p, i: (b, 0, 2 * n_pairs + p)),
                  pl.BlockSpec((SB_KB, SB_KB), lambda b, p, i: (0, 0))],
        out_specs=pl.BlockSpec((1, SB_QB, LANES), lambda b, p, i: (b, i, p)),
        out_shape=jax.ShapeDtypeStruct((b_, s_, width // 3), BF16),
        scratch_shapes=[pltpu.VMEM((2 * SB_QB, LANES), F32), pltpu.VMEM((2 * SB_QB, LANES), F32)],
        compiler_params=_params(("parallel", "parallel", "arbitrary")),
        name="sb_attn",
    )(qkv, qkv, qkv, tri)


def kernel(x, even_w_in, even_rel_bias, ssm_lambda_re, ssm_lambda_im, ssm_b_re, ssm_b_im,
           ssm_c_re, ssm_c_im, ssm_d, ssm_log_dt, ssm_glu_w, ssm_glu_b, even_w_out,
           odd_w_in, odd_w_out, ffn_w1, ffn_w3, ffn_w2, ln_mix_g, ln_mix_b, ln_ffn_g, ln_ffn_b):
    b_, s_, d = x.shape
    m = b_ * s_
    row = lambda v: v.astype(F32).reshape(1, -1)
    h = x.reshape(m, d).astype(F32)
    for layer in range(DEPTH):
        if layer % 2 == 0:
            e = layer // 2
            qkv, u = _matmul(h, even_w_in[e].astype(BF16),
                             [(3 * MIX_A_WIDTH, BF16), (MIX_B_WIDTH, F32)])
            qkv = qkv.reshape(b_, s_, 3 * MIX_A_WIDTH)
            u = u.reshape(b_, s_, MIX_B_WIDTH)
            oa = _chunk_attention(qkv, _band_bias(even_rel_bias[e]))
            bblk, ar, ai, cblk = _ssm_tables(ssm_lambda_re[e], ssm_lambda_im[e], ssm_b_re[e],
                                             ssm_b_im[e], ssm_c_re[e], ssm_c_im[e], ssm_log_dt[e])
            ob = _ssm_glu(u, bblk, ar, ai, cblk, row(ssm_d[e]), ssm_glu_w[e].astype(BF16),
                          row(ssm_glu_b[e]))
            w_out = even_w_out[e].astype(BF16)
            acts = [oa.reshape(m, MIX_A_WIDTH), ob.reshape(m, MIX_B_WIDTH)]
            weights = [w_out[:MIX_A_WIDTH], w_out[MIX_A_WIDTH:]]
        else:
            o = layer // 2
            (qkv,) = _matmul(h, odd_w_in[o].astype(BF16), [(3 * d, BF16)])
            qkv = qkv.reshape(b_, s_, 3 * d)
            ridx = jnp.arange(SB_KB)
            tri = jnp.where(ridx[:, None] >= ridx[None, :], -1.0, 0.0).astype(BF16)
            oc = _stick_breaking_attention(qkv, tri)
            acts = [oc.reshape(m, d)]
            weights = [odd_w_out[o].astype(BF16)]
        h = _proj_ln(acts, weights, h, row(ln_mix_g[layer]), row(ln_mix_b[layer]))
        h = _ffn_ln(h, ffn_w1[layer].astype(BF16), ffn_w3[layer].astype(BF16),
                    ffn_w2[layer].astype(BF16), row(ln_ffn_g[layer]), row(ln_ffn_b[layer]))
    return h.reshape(b_, s_, d).astype(x.dtype)
```

```python
import functools
import math

import jax
import jax.numpy as jnp
from jax import lax
from jax.experimental import pallas as pl
from jax.experimental.pallas import tpu as pltpu

D_MODEL = 1024
DEPTH = 2
CHUNK = 64
LEFT_CHUNKS = 8
HEAD_DIM = 64
MIX_A_WIDTH = 512
MIX_B_WIDTH = 512
A_HEADS = 8
MAX_REL = 128
SSM_GROUP = 16
SSM_GROUPS = 32
SSM_STATE = 64
C_HEADS = 16
DEEPNORM_ALPHA = (2 * DEPTH) ** 0.25
LN_EPS = 1e-5
ATTN_SCALE = HEAD_DIM ** -0.5
NEG_BIG = -1e30
LOG2E = 1.0 / math.log(2.0)
EXP_ZERO_BELOW = -104.0

LANES = 128
SUBLANES = 8
VMEM_LIMIT_BYTES = 56 * 1024 * 1024

MM_TM, MM_TN = 512, 512
LN_TM = 512
FFN_TM, FFN_TF = 512, 256
CA_QB = 256
CA_KB = 3 * CA_QB
CA_LANES = 256
SSM_T = 128
SSM_LC = 512
SB_QB = 512
SB_KB = 256

BF16 = jnp.bfloat16
F32 = jnp.float32


def _params(semantics):
    return pltpu.CompilerParams(dimension_semantics=semantics, vmem_limit_bytes=VMEM_LIMIT_BYTES)


def _layer_norm_rows(y, g, b):
    mu = jnp.mean(y, axis=-1, keepdims=True)
    yc = y - mu
    var = jnp.mean(yc * yc, axis=-1, keepdims=True)
    return yc * lax.rsqrt(var + LN_EPS) * g + b


def _matmul_kernel(x_ref, w_ref, *o_refs):
    xb = x_ref[...].astype(BF16)
    col = 0
    for o_ref in o_refs:
        width = o_ref.shape[1]
        for c in range(0, width, MM_TN):
            o_ref[:, c:c + MM_TN] = jnp.dot(xb, w_ref[:, col + c:col + c + MM_TN],
                                            preferred_element_type=F32).astype(o_ref.dtype)
        col += width


def _row_tiles(m, tm, width, seq_len, time_major):
    if not time_major:
        return (m, width), pl.BlockSpec((tm, width), lambda i: (i, 0))
    assert seq_len % tm == 0
    tiles_per_seq = seq_len // tm
    return ((seq_len, (m // seq_len) * width),
            pl.BlockSpec((tm, width), lambda i: (i % tiles_per_seq, i // tiles_per_seq)))


def _matmul(x, w, outs, seq_len):
    m, k = x.shape
    n = w.shape[1]
    tm = min(MM_TM, m)
    assert m % tm == 0 and sum(o[0] for o in outs) == n and all(o[0] % MM_TN == 0 for o in outs)
    tiles = [_row_tiles(m, tm, wd, seq_len, tmaj) for wd, _, tmaj in outs]
    return pl.pallas_call(
        _matmul_kernel,
        grid=(m // tm,),
        in_specs=[pl.BlockSpec((tm, k), lambda i: (i, 0)),
                  pl.BlockSpec((k, n), lambda i: (0, 0))],
        out_specs=[spec for _, spec in tiles],
        out_shape=[jax.ShapeDtypeStruct(shape, o[1]) for (shape, _), o in zip(tiles, outs)],
        compiler_params=_params(("parallel",)),
        name="proj_matmul",
    )(x, w)


def _proj_ln_kernel(*refs, n_in):
    a_refs = refs[:n_in]
    w_refs = refs[n_in:2 * n_in]
    h_ref, g_ref, b_ref, o_ref = refs[2 * n_in:]
    mix = jnp.dot(a_refs[0][...], w_refs[0][...], preferred_element_type=F32)
    for a_ref, w_ref in zip(a_refs[1:], w_refs[1:]):
        mix += jnp.dot(a_ref[...], w_ref[...], preferred_element_type=F32)
    y = DEEPNORM_ALPHA * h_ref[...] + mix
    o_ref[...] = _layer_norm_rows(y, g_ref[...], b_ref[...])


def _proj_ln(acts, weights, h, g, b, seq_len):
    m, d = h.shape
    tm = min(LN_TM, m)
    assert m % tm == 0
    n_in = len(acts)
    act_specs = []
    for (a, tmaj), w in zip(acts, weights):
        shape, spec = _row_tiles(m, tm, w.shape[0], seq_len, tmaj)
        assert a.shape == shape
        act_specs.append(spec)
    acts = [a for a, _ in acts]
    in_specs = (act_specs
                + [pl.BlockSpec(w.shape, lambda i: (0, 0)) for w in weights]
                + [pl.BlockSpec((tm, d), lambda i: (i, 0)),
                   pl.BlockSpec((1, d), lambda i: (0, 0)),
                   pl.BlockSpec((1, d), lambda i: (0, 0))])
    return pl.pallas_call(
        functools.partial(_proj_ln_kernel, n_in=n_in),
        grid=(m // tm,),
        in_specs=in_specs,
        out_specs=pl.BlockSpec((tm, d), lambda i: (i, 0)),
        out_shape=jax.ShapeDtypeStruct((m, d), F32),
        compiler_params=_params(("parallel",)),
        name="proj_ln",
    )(*acts, *weights, h, g, b)


def _ffn_kernel(x_ref, w1_ref, w3_ref, w2_ref, g_ref, b_ref, o_ref):
    x = x_ref[...]
    xb = x.astype(BF16)
    acc = DEEPNORM_ALPHA * x
    for f in range(0, w1_ref.shape[1], FFN_TF):
        h1 = jnp.dot(xb, w1_ref[:, f:f + FFN_TF], preferred_element_type=F32)
        h3 = jnp.dot(xb, w3_ref[:, f:f + FFN_TF], preferred_element_type=F32)
        gated = (h1 * jax.nn.sigmoid(h1) * h3).astype(BF16)
        acc += jnp.dot(gated, w2_ref[f:f + FFN_TF, :], preferred_element_type=F32)
    o_ref[...] = _layer_norm_rows(acc, g_ref[...], b_ref[...])


def _ffn_ln(x, w1, w3, w2, g, b):
    m, d = x.shape
    dff = w1.shape[1]
    tm = min(FFN_TM, m)
    assert m % tm == 0 and dff % FFN_TF == 0
    const = lambda shape: pl.BlockSpec(shape, lambda i: (0, 0))
    return pl.pallas_call(
        _ffn_kernel,
        grid=(m // tm,),
        in_specs=[pl.BlockSpec((tm, d), lambda i: (i, 0)),
                  const((d, dff)), const((d, dff)), const((dff, d)), const((1, d)), const((1, d))],
        out_specs=pl.BlockSpec((tm, d), lambda i: (i, 0)),
        out_shape=jax.ShapeDtypeStruct((m, d), F32),
        compiler_params=_params(("parallel",)),
        name="ffn_ln",
    )(x, w1, w3, w2, g, b)


def _chunk_attn_kernel(q_ref, k0_ref, k1_ref, k2_ref, v0_ref, v1_ref, v2_ref, bias_ref, o_ref):
    qb = pl.program_id(2)
    q = q_ref[0]
    kk = jnp.concatenate([k0_ref[0], k1_ref[0], k2_ref[0]], axis=0)
    vv = jnp.concatenate([v0_ref[0], v1_ref[0], v2_ref[0]], axis=0)
    lane = lax.broadcasted_iota(jnp.int32, q.shape, 1)
    qs = q * jnp.asarray(ATTN_SCALE, q.dtype)
    col = lax.broadcasted_iota(jnp.int32, (q.shape[0], kk.shape[0]), 1)
    col_valid = col >= (2 - qb) * q.shape[0]
    out = jnp.zeros(q.shape, F32)
    for hh in range(q.shape[1] // HEAD_DIM):
        in_head = (lane >= hh * HEAD_DIM) & (lane < (hh + 1) * HEAD_DIM)
        qm = jnp.where(in_head, qs, jnp.zeros_like(qs))
        s = lax.dot_general(qm, kk, (((1,), (1,)), ((), ())), preferred_element_type=F32)
        s = jnp.where(col_valid, s + bias_ref[hh], NEG_BIG)
        m = jnp.max(s, axis=-1, keepdims=True)
        e = jnp.exp(s - m)
        l = jnp.sum(e, axis=-1, keepdims=True)
        pv = jnp.dot(e.astype(BF16), vv, preferred_element_type=F32)
        out = jnp.where(in_head, pv / l, out)
    o_ref[0] = out.astype(o_ref.dtype)


def _chunk_attention(qkv, bias):
    b_, s_, _ = qkv.shape
    n_grp = MIX_A_WIDTH // CA_LANES
    nq = s_ // CA_QB

    def kv_spec(col0, back):
        return pl.BlockSpec((1, CA_QB, CA_LANES),
                            lambda b, p, i: (b, jnp.maximum(i - back, 0), col0 + p))

    return pl.pallas_call(
        _chunk_attn_kernel,
        grid=(b_, n_grp, nq),
        in_specs=[pl.BlockSpec((1, CA_QB, CA_LANES), lambda b, p, i: (b, i, p)),
                  kv_spec(n_grp, 2), kv_spec(n_grp, 1), kv_spec(n_grp, 0),
                  kv_spec(2 * n_grp, 2), kv_spec(2 * n_grp, 1), kv_spec(2 * n_grp, 0),
                  pl.BlockSpec((CA_LANES // HEAD_DIM, CA_QB, CA_KB), lambda b, p, i: (p, 0, 0))],
        out_specs=pl.BlockSpec((1, CA_QB, CA_LANES), lambda b, p, i: (b, i, p)),
        out_shape=jax.ShapeDtypeStruct((b_, s_, MIX_A_WIDTH), BF16),
        compiler_params=_params(("parallel", "parallel", "arbitrary")),
        name="chunk_attn",
    )(qkv, qkv, qkv, qkv, qkv, qkv, qkv, bias)


def _band_bias(rel_table):
    assert CA_KB == CA_QB + LEFT_CHUNKS * CHUNK
    qi = jnp.arange(CA_QB)
    ki = jnp.arange(CA_KB)
    dist = (qi[:, None] // CHUNK + LEFT_CHUNKS) - (ki[None, :] // CHUNK)
    in_band = (dist >= 0) & (dist <= LEFT_CHUNKS)
    span = CA_QB + CA_KB - 1
    lo_pad = CA_KB - 1 - LEFT_CHUNKS * CHUNK - MAX_REL
    hi_pad = span - lo_pad - (2 * MAX_REL + 1)
    ext = jnp.pad(rel_table.astype(F32), ((0, 0), (lo_pad, hi_pad)), mode='edge')
    r = jnp.pad(ext[:, ::-1], ((0, 0), (0, 1)))
    shifted = jnp.tile(r, (1, CA_QB))[:, :CA_QB * span].reshape(-1, CA_QB, span)
    bias = shifted[:, :, CA_QB - 1:CA_QB - 1 + CA_KB]
    return jnp.where(in_band[None], bias, NEG_BIG)


def _ssm_kernel(u_ref, bblk_ref, ar_ref, ai_ref, cblk_ref, d_ref, gw_ref, gb_ref, o_ref,
                bu_ref, st_ref):
    n_rows, width = u_ref.shape
    n_state = ar_ref.shape[1]
    n_tiles = n_state // LANES
    tiles_per_ch = n_tiles // (width // LANES)

    @pl.when(pl.program_id(0) == 0)
    def _():
        st_ref[...] = jnp.zeros_like(st_ref)

    u = u_ref[...]
    ub = u.astype(BF16)
    for half in range(2):
        for k in range(0, n_tiles, 2):
            ch = (k // tiles_per_ch) * LANES
            col = half * n_state + k * LANES
            res = jnp.dot(ub[:, ch:ch + LANES], bblk_ref[ch:ch + LANES, col:col + 2 * LANES],
                          preferred_element_type=F32)
            bu_ref[half * n_tiles + k] = res[:, :LANES]
            bu_ref[half * n_tiles + k + 1] = res[:, LANES:]

    group = SSM_LC // LANES
    half = SUBLANES // 2
    first = lax.broadcasted_iota(jnp.int32, (SUBLANES, LANES), 0) < half
    for c in range(0, n_tiles, group):
        ar = [jnp.broadcast_to(ar_ref[:, pl.ds((c + k) * LANES, LANES)], (SUBLANES, LANES))
              for k in range(group)]
        ai = [jnp.broadcast_to(ai_ref[:, pl.ds((c + k) * LANES, LANES)], (SUBLANES, LANES))
              for k in range(group)]

        def two_steps(kt, carry, c=c, ar=ar, ai=ai):
            rows = pl.ds(pl.multiple_of(kt * SUBLANES, SUBLANES), SUBLANES)
            out = []
            for k in range(group):
                xr, xi = carry[2 * k], carry[2 * k + 1]
                br = bu_ref[c + k, rows, :]
                bi = bu_ref[n_tiles + c + k, rows, :]
                xr, xi = pltpu.roll(xr, half, 0), pltpu.roll(xi, half, 0)
                er = ar[k] * xr - ai[k] * xi + br
                ei = ar[k] * xi + ai[k] * xr + bi
                xr, xi = pltpu.roll(er, half, 0), pltpu.roll(ei, half, 0)
                orr = ar[k] * xr - ai[k] * xi + br
                oi = ar[k] * xi + ai[k] * xr + bi
                bu_ref[c + k, rows, :] = jnp.where(first, er, orr)
                bu_ref[n_tiles + c + k, rows, :] = jnp.where(first, ei, oi)
                out += [orr, oi]
            return tuple(out)

        init = []
        for k in range(group):
            init += [st_ref[c + k], st_ref[n_tiles + c + k]]
        fin = lax.fori_loop(0, n_rows // SUBLANES, two_steps, tuple(init), unroll=4)
        for k in range(group):
            st_ref[c + k] = fin[2 * k]
            st_ref[n_tiles + c + k] = fin[2 * k + 1]

    ys = []
    for m in range(width // LANES):
        t0 = m * tiles_per_ch
        xs = [bu_ref[half * n_tiles + t0 + k] for half in range(2) for k in range(tiles_per_ch)]
        rows = [pl.ds(half * n_state + t0 * LANES, tiles_per_ch * LANES) for half in range(2)]
        cs = jnp.concatenate([cblk_ref[rows[0], m * LANES:(m + 1) * LANES],
                              cblk_ref[rows[1], m * LANES:(m + 1) * LANES]], axis=0)
        ys.append(jnp.dot(jnp.concatenate(xs, axis=1).astype(BF16), cs,
                          preferred_element_type=F32))
    y = jnp.concatenate(ys, axis=1)
    y = jax.nn.gelu(y + d_ref[...] * u)
    gate = jax.nn.sigmoid(jnp.dot(y.astype(BF16), gw_ref[...], preferred_element_type=F32)
                          + gb_ref[...])
    o_ref[...] = (y * gate).astype(o_ref.dtype)


def _ssm_glu(u_tm, b_, bblk, ar, ai, cblk, d_skip, glu_w, glu_b):
    width = u_tm.shape[1]
    s_ = u_tm.shape[0] // b_
    n_state = ar.shape[1]
    tt = min(SSM_T, s_)
    assert s_ % tt == 0 and n_state % SSM_LC == 0
    assert 2 * b_ == SUBLANES, "the scan packs two time steps of a batch of 4 into one 8-row tile"
    rows = tt * b_
    const = lambda shape: pl.BlockSpec(shape, lambda t: (0,) * len(shape))
    return pl.pallas_call(
        _ssm_kernel,
        grid=(s_ // tt,),
        in_specs=[pl.BlockSpec((rows, width), lambda t: (t, 0)),
                  const(bblk.shape), const(ar.shape), const(ai.shape), const(cblk.shape),
                  const(d_skip.shape), const(glu_w.shape), const(glu_b.shape)],
        out_specs=pl.BlockSpec((rows, width), lambda t: (t, 0)),
        out_shape=jax.ShapeDtypeStruct((s_ * b_, width), BF16),
        scratch_shapes=[pltpu.VMEM((2 * n_state // LANES, rows, LANES), F32),
                        pltpu.VMEM((2 * n_state // LANES, SUBLANES, LANES), F32)],
        compiler_params=_params(("arbitrary",)),
        name="ssm_glu",
    )(u_tm, bblk, ar, ai, cblk, d_skip, glu_w, glu_b)


def _ssm_tables(lam_re, lam_im, b_re, b_im, c_re, c_im, log_dt):
    dt = jnp.exp(log_dt.astype(F32))[:, None]
    lr, li = lam_re.astype(F32), lam_im.astype(F32)
    mag = jnp.exp(lr * dt)
    ar = mag * jnp.cos(li * dt)
    ai = mag * jnp.sin(li * dt)
    den = lr * lr + li * li
    fr = ((ar - 1.0) * lr + ai * li) / den
    fi = (ai * lr - (ar - 1.0) * li) / den
    br_, bi_ = b_re.astype(F32), b_im.astype(F32)
    bbar_r = fr[..., None] * br_ - fi[..., None] * bi_
    bbar_i = fr[..., None] * bi_ + fi[..., None] * br_
    eye = jnp.eye(SSM_GROUPS, dtype=F32)
    n_state = SSM_GROUPS * SSM_STATE

    def in_map(bb):
        return jnp.einsum('gpi,gh->gihp', bb, eye).reshape(MIX_B_WIDTH, n_state)

    def out_map(cc):
        return jnp.einsum('gip,gh->gphi', cc, eye).reshape(n_state, MIX_B_WIDTH)

    bblk = jnp.concatenate([in_map(bbar_r), in_map(bbar_i)], axis=1).astype(BF16)
    cblk = jnp.concatenate([out_map(c_re.astype(F32)), -out_map(c_im.astype(F32))],
                           axis=0).astype(BF16)
    return bblk, ar.reshape(1, n_state), ai.reshape(1, n_state), cblk


def _softplus(z):
    t = z * LOG2E
    neg_abs = lax.bitcast_convert_type(
        lax.bitcast_convert_type(t, jnp.uint32) | jnp.uint32(0x80000000), F32)
    return jnp.maximum(z, 0.0) + jnp.log(1.0 + jnp.exp2(neg_abs))


def _sb_attn_kernel(q_ref, k_ref, v_ref, tri_ref, lt_ref, o_ref, qst_ref, acc_ref, carry_ref):
    i = pl.program_id(2)
    lane = lax.broadcasted_iota(jnp.int32, (SB_KB, LANES), 1)
    for sub in range(2):
        qs = q_ref[0, sub * SB_KB:(sub + 1) * SB_KB, :]
        qs = qs * jnp.asarray(ATTN_SCALE, qs.dtype)
        zero = jnp.zeros_like(qs)
        qst_ref[(2 * sub) * SB_KB:(2 * sub + 1) * SB_KB, :] = jnp.where(lane < HEAD_DIM, qs, zero)
        qst_ref[(2 * sub + 1) * SB_KB:(2 * sub + 2) * SB_KB, :] = jnp.where(lane >= HEAD_DIM, qs,
                                                                            zero)
    acc_ref[...] = jnp.zeros_like(acc_ref)
    carry_ref[...] = jnp.zeros_like(carry_ref)
    n_rows = 4 * SB_KB

    def causal(tile):
        lt2 = jnp.concatenate([lt_ref[...], lt_ref[...]], axis=0)
        if tile.shape[0] == lt2.shape[0]:
            return tile * lt2
        return jnp.concatenate([tile[:lt2.shape[0]] * lt2, tile[lt2.shape[0]:]], axis=0)

    def key_block(j, r0, r1, diagonal, gate=None):
        start = pl.multiple_of(j * SB_KB, SB_KB)
        kblk = k_ref[0, pl.ds(start, SB_KB), :]
        vblk = v_ref[0, pl.ds(start, SB_KB), :]
        z = lax.dot_general(qst_ref[r0:r1, :], kblk, (((1,), (1,)), ((), ())),
                            preferred_element_type=F32)
        sp = _softplus(z)
        if diagonal:
            sp = causal(sp)
        c = jnp.dot(sp.astype(BF16), tri_ref[...], preferred_element_type=F32)
        if diagonal:
            w = causal(jnp.exp(jnp.minimum(z + c, 0.0)))
        else:
            w = jnp.exp(z + c)
        pv = jnp.dot(w.astype(BF16), vblk, preferred_element_type=F32)
        total = jnp.broadcast_to(c[:, 0:1], pv.shape)
        if gate is not None:
            pv, total = pv * gate, total * gate
        carry = carry_ref[r0:r1, :]
        acc_ref[r0:r1, :] += jnp.exp(carry) * pv
        carry_ref[r0:r1, :] = carry + total

    key_block(2 * i + 1, 2 * SB_KB, n_rows, True)
    key_block(2 * i, 0, n_rows, True)
    key_block(jnp.maximum(2 * i - 1, 0), 0, 2 * SB_KB, False, gate=(i > 0).astype(F32))

    def alive(r0, r1):
        return jnp.max(carry_ref[r0:r1, :]) >= EXP_ZERO_BELOW

    @pl.when((i > 0) & alive(2 * SB_KB, n_rows))
    def _():
        key_block(2 * i - 1, 2 * SB_KB, n_rows, False)

    def sweep(first_j, r0, r1, watch0, watch1):
        def cond(state):
            j, still_alive = state
            return (j >= 0) & still_alive

        def body(state):
            j, _ = state
            key_block(j, r0, r1, False)
            return j - 1, alive(watch0, watch1)

        return lax.while_loop(cond, body, (first_j, alive(watch0, watch1)))[0]

    j_next = sweep(2 * i - 2, 0, n_rows, 2 * SB_KB, n_rows)
    sweep(j_next, 0, 2 * SB_KB, 0, 2 * SB_KB)

    for sub in range(2):
        a0 = acc_ref[(2 * sub) * SB_KB:(2 * sub + 1) * SB_KB, :]
        a1 = acc_ref[(2 * sub + 1) * SB_KB:(2 * sub + 2) * SB_KB, :]
        o_ref[0, sub * SB_KB:(sub + 1) * SB_KB, :] = jnp.where(lane < HEAD_DIM, a0,
                                                               a1).astype(o_ref.dtype)


def _stick_breaking_attention(qkv):
    b_, s_, width = qkv.shape
    n_pairs = width // 3 // LANES
    assert SB_QB == 2 * SB_KB and s_ % SB_QB == 0
    ridx = jnp.arange(SB_KB)
    tri = jnp.where(ridx[:, None] >= ridx[None, :], -1.0, 0.0).astype(BF16)
    lt = jnp.where(ridx[None, :] < ridx[:, None], 1.0, 0.0).astype(F32)
    return pl.pallas_call(
        _sb_attn_kernel,
        grid=(b_, n_pairs, s_ // SB_QB),
        in_specs=[pl.BlockSpec((1, SB_QB, LANES), lambda b, p, i: (b, i, p)),
                  pl.BlockSpec((1, s_, LANES), lambda b, p, i: (b, 0, n_pairs + p)),
                  pl.BlockSpec((1, s_, LANES), lambda b, p, i: (b, 0, 2 * n_pairs + p)),
                  pl.BlockSpec((SB_KB, SB_KB), lambda b, p, i: (0, 0)),
                  pl.BlockSpec((SB_KB, SB_KB), lambda b, p, i: (0, 0))],
        out_specs=pl.BlockSpec((1, SB_QB, LANES), lambda b, p, i: (b, i, p)),
        out_shape=jax.ShapeDtypeStruct((b_, s_, width // 3), BF16),
        scratch_shapes=[pltpu.VMEM((2 * SB_QB, LANES), BF16), pltpu.VMEM((2 * SB_QB, LANES), F32),
                        pltpu.VMEM((2 * SB_QB, LANES), F32)],
        compiler_params=_params(("parallel", "parallel", "arbitrary")),
        name="sb_attn",
    )(qkv, qkv, qkv, tri, lt)


def kernel(x, even_w_in, even_rel_bias, ssm_lambda_re, ssm_lambda_im, ssm_b_re, ssm_b_im,
           ssm_c_re, ssm_c_im, ssm_d, ssm_log_dt, ssm_glu_w, ssm_glu_b, even_w_out,
           odd_w_in, odd_w_out, ffn_w1, ffn_w3, ffn_w2, ln_mix_g, ln_mix_b, ln_ffn_g, ln_ffn_b):
    b_, s_, d = x.shape
    m = b_ * s_
    row = lambda v: v.astype(F32).reshape(1, -1)
    h = x.reshape(m, d).astype(F32)
    for layer in range(DEPTH):
        if layer % 2 == 0:
            e = layer // 2
            qkv, u_tm = _matmul(h, even_w_in[e].astype(BF16),
                                [(3 * MIX_A_WIDTH, BF16, False), (MIX_B_WIDTH, F32, True)], s_)
            qkv = qkv.reshape(b_, s_, 3 * MIX_A_WIDTH)
            oa = _chunk_attention(qkv, _band_bias(even_rel_bias[e]))
            bblk, ar, ai, cblk = _ssm_tables(ssm_lambda_re[e], ssm_lambda_im[e], ssm_b_re[e],
                                             ssm_b_im[e], ssm_c_re[e], ssm_c_im[e], ssm_log_dt[e])
            ob_tm = _ssm_glu(u_tm.reshape(s_ * b_, MIX_B_WIDTH), b_, bblk, ar, ai, cblk,
                             row(ssm_d[e]), ssm_glu_w[e].astype(BF16), row(ssm_glu_b[e]))
            w_out = even_w_out[e].astype(BF16)
            acts = [(oa.reshape(m, MIX_A_WIDTH), False),
                    (ob_tm.reshape(s_, b_ * MIX_B_WIDTH), True)]
            weights = [w_out[:MIX_A_WIDTH], w_out[MIX_A_WIDTH:]]
        else:
            o = layer // 2
            (qkv,) = _matmul(h, odd_w_in[o].astype(BF16), [(3 * d, BF16, False)], s_)
            qkv = qkv.reshape(b_, s_, 3 * d)
            oc = _stick_breaking_attention(qkv)
            acts = [(oc.reshape(m, d), False)]
            weights = [odd_w_out[o].astype(BF16)]
        h = _proj_ln(acts, weights, h, row(ln_mix_g[layer]), row(ln_mix_b[layer]), s_)
        h = _ffn_ln(h, ffn_w1[layer].astype(BF16), ffn_w3[layer].astype(BF16),
                    ffn_w2[layer].astype(BF16), row(ln_ffn_g[layer]), row(ln_ffn_b[layer]))
    return h.reshape(b_, s_, d).astype(x.dtype)
```

```python
import functools
import math

import jax
import jax.numpy as jnp
from jax import lax
from jax.experimental import pallas as pl
from jax.experimental.pallas import tpu as pltpu

D_MODEL = 1024
DEPTH = 2
CHUNK = 64
LEFT_CHUNKS = 8
HEAD_DIM = 64
MIX_A_WIDTH = 512
MIX_B_WIDTH = 512
A_HEADS = 8
MAX_REL = 128
SSM_GROUP = 16
SSM_GROUPS = 32
SSM_STATE = 64
C_HEADS = 16
DEEPNORM_ALPHA = (2 * DEPTH) ** 0.25
LN_EPS = 1e-5
ATTN_SCALE = HEAD_DIM ** -0.5
NEG_BIG = -1e30
LOG2E = 1.0 / math.log(2.0)
EXP_ZERO_BELOW = -104.0

LANES = 128
SUBLANES = 8
VMEM_LIMIT_BYTES = 56 * 1024 * 1024

MM_TM, MM_TN = 512, 512
FFN_TM, FFN_TF = 512, 256
CA_QB = 256
CA_KB = 3 * CA_QB
CA_LANES = 256
SSM_T = 128
SSM_LC = 512
SB_QB = 512
SB_KB = 256

BF16 = jnp.bfloat16
F32 = jnp.float32


def _params(semantics):
    return pltpu.CompilerParams(dimension_semantics=semantics, vmem_limit_bytes=VMEM_LIMIT_BYTES)


def _layer_norm_rows(y, g, b):
    mu = jnp.mean(y, axis=-1, keepdims=True)
    yc = y - mu
    var = jnp.mean(yc * yc, axis=-1, keepdims=True)
    return yc * lax.rsqrt(var + LN_EPS) * g + b


def _matmul_kernel(x_ref, w_ref, *o_refs):
    xb = x_ref[...].astype(BF16)
    col = 0
    for o_ref in o_refs:
        width = o_ref.shape[1]
        for c in range(0, width, MM_TN):
            o_ref[:, c:c + MM_TN] = jnp.dot(xb, w_ref[:, col + c:col + c + MM_TN],
                                            preferred_element_type=F32).astype(o_ref.dtype)
        col += width


def _row_tiles(m, tm, width, seq_len, time_major):
    if not time_major:
        return (m, width), pl.BlockSpec((tm, width), lambda i: (i, 0))
    assert seq_len % tm == 0
    tiles_per_seq = seq_len // tm
    return ((seq_len, (m // seq_len) * width),
            pl.BlockSpec((tm, width), lambda i: (i % tiles_per_seq, i // tiles_per_seq)))


def _matmul(x, w, outs, seq_len):
    m, k = x.shape
    n = w.shape[1]
    tm = min(MM_TM, m)
    assert m % tm == 0 and sum(o[0] for o in outs) == n and all(o[0] % MM_TN == 0 for o in outs)
    tiles = [_row_tiles(m, tm, wd, seq_len, tmaj) for wd, _, tmaj in outs]
    return pl.pallas_call(
        _matmul_kernel,
        grid=(m // tm,),
        in_specs=[pl.BlockSpec((tm, k), lambda i: (i, 0)),
                  pl.BlockSpec((k, n), lambda i: (0, 0))],
        out_specs=[spec for _, spec in tiles],
        out_shape=[jax.ShapeDtypeStruct(shape, o[1]) for (shape, _), o in zip(tiles, outs)],
        compiler_params=_params(("parallel",)),
        name="proj_matmul",
    )(x, w)


def _mix_ffn_kernel(*refs, n_in):
    a_refs = refs[:n_in]
    wo_refs = refs[n_in:2 * n_in]
    h_ref, g1_ref, b1_ref, w1_ref, w3_ref, w2_ref, g2_ref, b2_ref, o_ref = refs[2 * n_in:]
    mix = jnp.dot(a_refs[0][...], wo_refs[0][...], preferred_element_type=F32)
    for a_ref, w_ref in zip(a_refs[1:], wo_refs[1:]):
        mix += jnp.dot(a_ref[...], w_ref[...], preferred_element_type=F32)
    x = _layer_norm_rows(DEEPNORM_ALPHA * h_ref[...] + mix, g1_ref[...], b1_ref[...])
    xb = x.astype(BF16)
    acc = DEEPNORM_ALPHA * x
    for f in range(0, w1_ref.shape[1], FFN_TF):
        h1 = jnp.dot(xb, w1_ref[:, f:f + FFN_TF], preferred_element_type=F32)
        h3 = jnp.dot(xb, w3_ref[:, f:f + FFN_TF], preferred_element_type=F32)
        gated = (h1 * jax.nn.sigmoid(h1) * h3).astype(BF16)
        acc += jnp.dot(gated, w2_ref[f:f + FFN_TF, :], preferred_element_type=F32)
    o_ref[...] = _layer_norm_rows(acc, g2_ref[...], b2_ref[...])


def _mix_ffn(acts, w_outs, h, g1, b1, w1, w3, w2, g2, b2, seq_len):
    m, d = h.shape
    dff = w1.shape[1]
    tm = min(FFN_TM, m)
    assert m % tm == 0 and dff % FFN_TF == 0
    n_in = len(acts)
    act_specs = []
    for (a, tmaj), w in zip(acts, w_outs):
        shape, spec = _row_tiles(m, tm, w.shape[0], seq_len, tmaj)
        assert a.shape == shape
        act_specs.append(spec)
    acts = [a for a, _ in acts]
    const = lambda arr: pl.BlockSpec(arr.shape, lambda i: (0, 0), pipeline_mode=pl.Buffered(1))
    row_tile = pl.BlockSpec((tm, d), lambda i: (i, 0))
    return pl.pallas_call(
        functools.partial(_mix_ffn_kernel, n_in=n_in),
        grid=(m // tm,),
        in_specs=(act_specs + [const(w) for w in w_outs]
                  + [row_tile, const(g1), const(b1), const(w1), const(w3), const(w2),
                     const(g2), const(b2)]),
        out_specs=row_tile,
        out_shape=jax.ShapeDtypeStruct((m, d), F32),
        compiler_params=_params(("parallel",)),
        name="mix_ffn",
    )(*acts, *w_outs, h, g1, b1, w1, w3, w2, g2, b2)


def _chunk_attn_kernel(q_ref, k0_ref, k1_ref, k2_ref, v0_ref, v1_ref, v2_ref, bias_ref, o_ref):
    qb = pl.program_id(2)
    q = q_ref[0]
    kk = jnp.concatenate([k0_ref[0], k1_ref[0], k2_ref[0]], axis=0)
    vv = jnp.concatenate([v0_ref[0], v1_ref[0], v2_ref[0]], axis=0)
    lane = lax.broadcasted_iota(jnp.int32, q.shape, 1)
    qs = q * jnp.asarray(ATTN_SCALE, q.dtype)
    col = lax.broadcasted_iota(jnp.int32, (q.shape[0], kk.shape[0]), 1)
    col_valid = col >= (2 - qb) * q.shape[0]
    out = jnp.zeros(q.shape, F32)
    for hh in range(q.shape[1] // HEAD_DIM):
        in_head = (lane >= hh * HEAD_DIM) & (lane < (hh + 1) * HEAD_DIM)
        qm = jnp.where(in_head, qs, jnp.zeros_like(qs))
        s = lax.dot_general(qm, kk, (((1,), (1,)), ((), ())), preferred_element_type=F32)
        s = jnp.where(col_valid, s + bias_ref[hh], NEG_BIG)
        m = jnp.max(s, axis=-1, keepdims=True)
        e = jnp.exp(s - m)
        l = jnp.sum(e, axis=-1, keepdims=True)
        pv = jnp.dot(e.astype(BF16), vv, preferred_element_type=F32)
        out = jnp.where(in_head, pv / l, out)
    o_ref[0] = out.astype(o_ref.dtype)


def _chunk_attention(qkv, bias):
    b_, s_, _ = qkv.shape
    n_grp = MIX_A_WIDTH // CA_LANES
    nq = s_ // CA_QB

    def kv_spec(col0, back):
        return pl.BlockSpec((1, CA_QB, CA_LANES),
                            lambda b, p, i: (b, jnp.maximum(i - back, 0), col0 + p))

    return pl.pallas_call(
        _chunk_attn_kernel,
        grid=(b_, n_grp, nq),
        in_specs=[pl.BlockSpec((1, CA_QB, CA_LANES), lambda b, p, i: (b, i, p)),
                  kv_spec(n_grp, 2), kv_spec(n_grp, 1), kv_spec(n_grp, 0),
                  kv_spec(2 * n_grp, 2), kv_spec(2 * n_grp, 1), kv_spec(2 * n_grp, 0),
                  pl.BlockSpec((CA_LANES // HEAD_DIM, CA_QB, CA_KB), lambda b, p, i: (p, 0, 0))],
        out_specs=pl.BlockSpec((1, CA_QB, CA_LANES), lambda b, p, i: (b, i, p)),
        out_shape=jax.ShapeDtypeStruct((b_, s_, MIX_A_WIDTH), BF16),
        compiler_params=_params(("parallel", "parallel", "arbitrary")),
        name="chunk_attn",
    )(qkv, qkv, qkv, qkv, qkv, qkv, qkv, bias)


def _band_bias(rel_table):
    assert CA_KB == CA_QB + LEFT_CHUNKS * CHUNK
    qi = jnp.arange(CA_QB)
    ki = jnp.arange(CA_KB)
    dist = (qi[:, None] // CHUNK + LEFT_CHUNKS) - (ki[None, :] // CHUNK)
    in_band = (dist >= 0) & (dist <= LEFT_CHUNKS)
    span = CA_QB + CA_KB - 1
    lo_pad = CA_KB - 1 - LEFT_CHUNKS * CHUNK - MAX_REL
    hi_pad = span - lo_pad - (2 * MAX_REL + 1)
    ext = jnp.pad(rel_table.astype(F32), ((0, 0), (lo_pad, hi_pad)), mode='edge')
    r = jnp.pad(ext[:, ::-1], ((0, 0), (0, 1)))
    shifted = jnp.tile(r, (1, CA_QB))[:, :CA_QB * span].reshape(-1, CA_QB, span)
    bias = shifted[:, :, CA_QB - 1:CA_QB - 1 + CA_KB]
    return jnp.where(in_band[None], bias, NEG_BIG)


def _ssm_kernel(u_ref, bblk_ref, ar_ref, ai_ref, cblk_ref, d_ref, gw_ref, gb_ref, o_ref,
                bu_ref, st_ref):
    n_rows, width = u_ref.shape
    n_state = ar_ref.shape[1]
    n_tiles = n_state // LANES
    tiles_per_ch = n_tiles // (width // LANES)

    @pl.when(pl.program_id(0) == 0)
    def _():
        st_ref[...] = jnp.zeros_like(st_ref)

    u = u_ref[...]
    ub = u.astype(BF16)
    for half in range(2):
        for k in range(0, n_tiles, 2):
            ch = (k // tiles_per_ch) * LANES
            col = half * n_state + k * LANES
            res = jnp.dot(ub[:, ch:ch + LANES], bblk_ref[ch:ch + LANES, col:col + 2 * LANES],
                          preferred_element_type=F32)
            bu_ref[half * n_tiles + k] = res[:, :LANES]
            bu_ref[half * n_tiles + k + 1] = res[:, LANES:]

    group = SSM_LC // LANES
    half = SUBLANES // 2
    first = lax.broadcasted_iota(jnp.int32, (SUBLANES, LANES), 0) < half
    for c in range(0, n_tiles, group):
        ar = [jnp.broadcast_to(ar_ref[:, pl.ds((c + k) * LANES, LANES)], (SUBLANES, LANES))
              for k in range(group)]
        ai = [jnp.broadcast_to(ai_ref[:, pl.ds((c + k) * LANES, LANES)], (SUBLANES, LANES))
              for k in range(group)]

        def two_steps(kt, carry, c=c, ar=ar, ai=ai):
            rows = pl.ds(pl.multiple_of(kt * SUBLANES, SUBLANES), SUBLANES)
            out = []
            for k in range(group):
                xr, xi = carry[2 * k], carry[2 * k + 1]
                br = bu_ref[c + k, rows, :]
                bi = bu_ref[n_tiles + c + k, rows, :]
                xr, xi = pltpu.roll(xr, half, 0), pltpu.roll(xi, half, 0)
                er = ar[k] * xr - ai[k] * xi + br
                ei = ar[k] * xi + ai[k] * xr + bi
                xr, xi = pltpu.roll(er, half, 0), pltpu.roll(ei, half, 0)
                orr = ar[k] * xr - ai[k] * xi + br
                oi = ar[k] * xi + ai[k] * xr + bi
                bu_ref[c + k, rows, :] = jnp.where(first, er, orr)
                bu_ref[n_tiles + c + k, rows, :] = jnp.where(first, ei, oi)
                out += [orr, oi]
            return tuple(out)

        init = []
        for k in range(group):
            init += [st_ref[c + k], st_ref[n_tiles + c + k]]
        fin = lax.fori_loop(0, n_rows // SUBLANES, two_steps, tuple(init), unroll=4)
        for k in range(group):
            st_ref[c + k] = fin[2 * k]
            st_ref[n_tiles + c + k] = fin[2 * k + 1]

    ys = []
    for m in range(width // LANES):
        t0 = m * tiles_per_ch
        xs = [bu_ref[half * n_tiles + t0 + k] for half in range(2) for k in range(tiles_per_ch)]
        rows = [pl.ds(half * n_state + t0 * LANES, tiles_per_ch * LANES) for half in range(2)]
        cs = jnp.concatenate([cblk_ref[rows[0], m * LANES:(m + 1) * LANES],
                              cblk_ref[rows[1], m * LANES:(m + 1) * LANES]], axis=0)
        ys.append(jnp.dot(jnp.concatenate(xs, axis=1).astype(BF16), cs,
                          preferred_element_type=F32))
    y = jnp.concatenate(ys, axis=1)
    y = jax.nn.gelu(y + d_ref[...] * u)
    gate = jax.nn.sigmoid(jnp.dot(y.astype(BF16), gw_ref[...], preferred_element_type=F32)
                          + gb_ref[...])
    o_ref[...] = (y * gate).astype(o_ref.dtype)


def _ssm_glu(u_tm, b_, bblk, ar, ai, cblk, d_skip, glu_w, glu_b):
    width = u_tm.shape[1]
    s_ = u_tm.shape[0] // b_
    n_state = ar.shape[1]
    tt = min(SSM_T, s_)
    assert s_ % tt == 0 and n_state % SSM_LC == 0
    assert 2 * b_ == SUBLANES, "the scan packs two time steps of a batch of 4 into one 8-row tile"
    rows = tt * b_
    const = lambda shape: pl.BlockSpec(shape, lambda t: (0,) * len(shape))
    return pl.pallas_call(
        _ssm_kernel,
        grid=(s_ // tt,),
        in_specs=[pl.BlockSpec((rows, width), lambda t: (t, 0)),
                  const(bblk.shape), const(ar.shape), const(ai.shape), const(cblk.shape),
                  const(d_skip.shape), const(glu_w.shape), const(glu_b.shape)],
        out_specs=pl.BlockSpec((rows, width), lambda t: (t, 0)),
        out_shape=jax.ShapeDtypeStruct((s_ * b_, width), BF16),
        scratch_shapes=[pltpu.VMEM((2 * n_state // LANES, rows, LANES), F32),
                        pltpu.VMEM((2 * n_state // LANES, SUBLANES, LANES), F32)],
        compiler_params=_params(("arbitrary",)),
        name="ssm_glu",
    )(u_tm, bblk, ar, ai, cblk, d_skip, glu_w, glu_b)


def _ssm_tables(lam_re, lam_im, b_re, b_im, c_re, c_im, log_dt):
    dt = jnp.exp(log_dt.astype(F32))[:, None]
    lr, li = lam_re.astype(F32), lam_im.astype(F32)
    mag = jnp.exp(lr * dt)
    ar = mag * jnp.cos(li * dt)
    ai = mag * jnp.sin(li * dt)
    den = lr * lr + li * li
    fr = ((ar - 1.0) * lr + ai * li) / den
    fi = (ai * lr - (ar - 1.0) * li) / den
    br_, bi_ = b_re.astype(F32), b_im.astype(F32)
    bbar_r = fr[..., None] * br_ - fi[..., None] * bi_
    bbar_i = fr[..., None] * bi_ + fi[..., None] * br_
    eye = jnp.eye(SSM_GROUPS, dtype=F32)
    n_state = SSM_GROUPS * SSM_STATE

    def in_map(bb):
        return jnp.einsum('gpi,gh->gihp', bb, eye).reshape(MIX_B_WIDTH, n_state)

    def out_map(cc):
        return jnp.einsum('gip,gh->gphi', cc, eye).reshape(n_state, MIX_B_WIDTH)

    bblk = jnp.concatenate([in_map(bbar_r), in_map(bbar_i)], axis=1).astype(BF16)
    cblk = jnp.concatenate([out_map(c_re.astype(F32)), -out_map(c_im.astype(F32))],
                           axis=0).astype(BF16)
    return bblk, ar.reshape(1, n_state), ai.reshape(1, n_state), cblk


def _softplus(z):
    return jnp.maximum(z, 0.0) + jnp.log(1.0 + jnp.exp2(-jnp.abs(z * LOG2E)))


def _sb_attn_kernel(q_ref, k_ref, v_ref, tri_ref, lt_ref, o_ref, qst_ref, acc_ref, carry_ref):
    i = pl.program_id(2)
    lane = lax.broadcasted_iota(jnp.int32, (SB_KB, LANES), 1)
    for sub in range(2):
        qs = q_ref[0, sub * SB_KB:(sub + 1) * SB_KB, :]
        qs = qs * jnp.asarray(ATTN_SCALE, qs.dtype)
        zero = jnp.zeros_like(qs)
        qst_ref[(2 * sub) * SB_KB:(2 * sub + 1) * SB_KB, :] = jnp.where(lane < HEAD_DIM, qs, zero)
        qst_ref[(2 * sub + 1) * SB_KB:(2 * sub + 2) * SB_KB, :] = jnp.where(lane >= HEAD_DIM, qs,
                                                                            zero)
    acc_ref[...] = jnp.zeros_like(acc_ref)
    carry_ref[...] = jnp.zeros_like(carry_ref)
    n_rows = 4 * SB_KB

    def causal(tile):
        lt2 = jnp.concatenate([lt_ref[...], lt_ref[...]], axis=0)
        if tile.shape[0] == lt2.shape[0]:
            return tile * lt2
        return jnp.concatenate([tile[:lt2.shape[0]] * lt2, tile[lt2.shape[0]:]], axis=0)

    def key_block(j, r0, r1, diagonal, gate=None):
        start = pl.multiple_of(j * SB_KB, SB_KB)
        kblk = k_ref[0, pl.ds(start, SB_KB), :]
        vblk = v_ref[0, pl.ds(start, SB_KB), :]
        z = lax.dot_general(qst_ref[r0:r1, :], kblk, (((1,), (1,)), ((), ())),
                            preferred_element_type=F32)
        sp = _softplus(z)
        if diagonal:
            sp = causal(sp)
        c = jnp.dot(sp.astype(BF16), tri_ref[...], preferred_element_type=F32)
        if diagonal:
            w = causal(jnp.exp(jnp.minimum(z + c, 0.0)))
        else:
            w = jnp.exp(z + c)
        pv = jnp.dot(w.astype(BF16), vblk, preferred_element_type=F32)
        total = jnp.broadcast_to(c[:, 0:1], pv.shape)
        if gate is not None:
            pv, total = pv * gate, total * gate
        carry = carry_ref[r0:r1, :]
        acc_ref[r0:r1, :] += jnp.exp(carry) * pv
        carry_ref[r0:r1, :] = carry + total

    key_block(2 * i + 1, 2 * SB_KB, n_rows, True)
    key_block(2 * i, 0, n_rows, True)
    key_block(jnp.maximum(2 * i - 1, 0), 0, 2 * SB_KB, False, gate=(i > 0).astype(F32))

    def alive(r0, r1):
        return jnp.max(carry_ref[r0:r1, :]) >= EXP_ZERO_BELOW

    @pl.when((i > 0) & alive(2 * SB_KB, n_rows))
    def _():
        key_block(2 * i - 1, 2 * SB_KB, n_rows, False)

    def sweep(first_j, r0, r1, watch0, watch1):
        def cond(state):
            j, still_alive = state
            return (j >= 0) & still_alive

        def body(state):
            j, _ = state
            key_block(j, r0, r1, False)
            return j - 1, alive(watch0, watch1)

        return lax.while_loop(cond, body, (first_j, alive(watch0, watch1)))[0]

    j_next = sweep(2 * i - 2, 0, n_rows, 2 * SB_KB, n_rows)
    sweep(j_next, 0, 2 * SB_KB, 0, 2 * SB_KB)

    for sub in range(2):
        a0 = acc_ref[(2 * sub) * SB_KB:(2 * sub + 1) * SB_KB, :]
        a1 = acc_ref[(2 * sub + 1) * SB_KB:(2 * sub + 2) * SB_KB, :]
        o_ref[0, sub * SB_KB:(sub + 1) * SB_KB, :] = jnp.where(lane < HEAD_DIM, a0,
                                                               a1).astype(o_ref.dtype)


def _stick_breaking_attention(qkv):
    b_, s_, width = qkv.shape
    n_pairs = width // 3 // LANES
    assert SB_QB == 2 * SB_KB and s_ % SB_QB == 0
    ridx = jnp.arange(SB_KB)
    tri = jnp.where(ridx[:, None] >= ridx[None, :], -1.0, 0.0).astype(BF16)
    lt = jnp.where(ridx[None, :] < ridx[:, None], 1.0, 0.0).astype(F32)
    return pl.pallas_call(
        _sb_attn_kernel,
        grid=(b_, n_pairs, s_ // SB_QB),
        in_specs=[pl.BlockSpec((1, SB_QB, LANES), lambda b, p, i: (b, i, p)),
                  pl.BlockSpec((1, s_, LANES), lambda b, p, i: (b, 0, n_pairs + p)),
                  pl.BlockSpec((1, s_, LANES), lambda b, p, i: (b, 0, 2 * n_pairs + p)),
                  pl.BlockSpec((SB_KB, SB_KB), lambda b, p, i: (0, 0)),
                  pl.BlockSpec((SB_KB, SB_KB), lambda b, p, i: (0, 0))],
        out_specs=pl.BlockSpec((1, SB_QB, LANES), lambda b, p, i: (b, i, p)),
        out_shape=jax.ShapeDtypeStruct((b_, s_, width // 3), BF16),
        scratch_shapes=[pltpu.VMEM((2 * SB_QB, LANES), BF16), pltpu.VMEM((2 * SB_QB, LANES), F32),
                        pltpu.VMEM((2 * SB_QB, LANES), F32)],
        compiler_params=_params(("parallel", "parallel", "arbitrary")),
        name="sb_attn",
    )(qkv, qkv, qkv, tri, lt)


def kernel(x, even_w_in, even_rel_bias, ssm_lambda_re, ssm_lambda_im, ssm_b_re, ssm_b_im,
           ssm_c_re, ssm_c_im, ssm_d, ssm_log_dt, ssm_glu_w, ssm_glu_b, even_w_out,
           odd_w_in, odd_w_out, ffn_w1, ffn_w3, ffn_w2, ln_mix_g, ln_mix_b, ln_ffn_g, ln_ffn_b):
    b_, s_, d = x.shape
    m = b_ * s_
    row = lambda v: v.astype(F32).reshape(1, -1)
    h = x.reshape(m, d).astype(F32)
    for layer in range(DEPTH):
        if layer % 2 == 0:
            e = layer // 2
            qkv, u_tm = _matmul(h, even_w_in[e].astype(BF16),
                                [(3 * MIX_A_WIDTH, BF16, False), (MIX_B_WIDTH, F32, True)], s_)
            qkv = qkv.reshape(b_, s_, 3 * MIX_A_WIDTH)
            oa = _chunk_attention(qkv, _band_bias(even_rel_bias[e]))
            bblk, ar, ai, cblk = _ssm_tables(ssm_lambda_re[e], ssm_lambda_im[e], ssm_b_re[e],
                                             ssm_b_im[e], ssm_c_re[e], ssm_c_im[e], ssm_log_dt[e])
            ob_tm = _ssm_glu(u_tm.reshape(s_ * b_, MIX_B_WIDTH), b_, bblk, ar, ai, cblk,
                             row(ssm_d[e]), ssm_glu_w[e].astype(BF16), row(ssm_glu_b[e]))
            w_out = even_w_out[e].astype(BF16)
            acts = [(oa.reshape(m, MIX_A_WIDTH), False),
                    (ob_tm.reshape(s_, b_ * MIX_B_WIDTH), True)]
            weights = [w_out[:MIX_A_WIDTH], w_out[MIX_A_WIDTH:]]
        else:
            o = layer // 2
            (qkv,) = _matmul(h, odd_w_in[o].astype(BF16), [(3 * d, BF16, False)], s_)
            qkv = qkv.reshape(b_, s_, 3 * d)
            oc = _stick_breaking_attention(qkv)
            acts = [(oc.reshape(m, d), False)]
            weights = [odd_w_out[o].astype(BF16)]
        h = _mix_ffn(acts, weights, h, row(ln_mix_g[layer]), row(ln_mix_b[layer]),
                     ffn_w1[layer].astype(BF16), ffn_w3[layer].astype(BF16),
                     ffn_w2[layer].astype(BF16), row(ln_ffn_g[layer]), row(ln_ffn_b[layer]), s_)
    return h.reshape(b_, s_, d).astype(x.dtype)
```

```python
import functools
import math

import jax
import jax.numpy as jnp
from jax import lax
from jax.experimental import pallas as pl
from jax.experimental.pallas import tpu as pltpu

D_MODEL = 1024
DEPTH = 2
CHUNK = 64
LEFT_CHUNKS = 8
HEAD_DIM = 64
MIX_A_WIDTH = 512
MIX_B_WIDTH = 512
A_HEADS = 8
MAX_REL = 128
SSM_GROUP = 16
SSM_GROUPS = 32
SSM_STATE = 64
C_HEADS = 16
DEEPNORM_ALPHA = (2 * DEPTH) ** 0.25
LN_EPS = 1e-5
ATTN_SCALE = HEAD_DIM ** -0.5
NEG_BIG = -1e30
LOG2E = 1.0 / math.log(2.0)
EXP_ZERO_BELOW = -104.0

LANES = 128
SUBLANES = 8
VMEM_LIMIT_BYTES = 56 * 1024 * 1024

MM_TM, MM_TN = 512, 512
FFN_TM, FFN_TF = 512, 256
CA_QB = 256
CA_KB = 3 * CA_QB
CA_LANES = 256
SSM_T = 128
SSM_LC = 512
SB_QB = 512
SB_KB = 256

BF16 = jnp.bfloat16
F32 = jnp.float32


def _params(semantics):
    return pltpu.CompilerParams(dimension_semantics=semantics, vmem_limit_bytes=VMEM_LIMIT_BYTES)


def _layer_norm_rows(y, g, b):
    mu = jnp.mean(y, axis=-1, keepdims=True)
    yc = y - mu
    var = jnp.mean(yc * yc, axis=-1, keepdims=True)
    return yc * lax.rsqrt(var + LN_EPS) * g + b


def _matmul_kernel(x_ref, w_ref, *o_refs):
    xb = x_ref[...].astype(BF16)
    col = 0
    for o_ref in o_refs:
        width = o_ref.shape[1]
        for c in range(0, width, MM_TN):
            o_ref[:, c:c + MM_TN] = jnp.dot(xb, w_ref[:, col + c:col + c + MM_TN],
                                            preferred_element_type=F32).astype(o_ref.dtype)
        col += width


def _row_tiles(m, tm, width, seq_len, time_major):
    if not time_major:
        return (m, width), pl.BlockSpec((tm, width), lambda i: (i, 0))
    assert seq_len % tm == 0
    tiles_per_seq = seq_len // tm
    return ((seq_len, (m // seq_len) * width),
            pl.BlockSpec((tm, width), lambda i: (i % tiles_per_seq, i // tiles_per_seq)))


def _matmul(x, w, outs, seq_len):
    m, k = x.shape
    n = w.shape[1]
    tm = min(MM_TM, m)
    assert m % tm == 0 and sum(o[0] for o in outs) == n and all(o[0] % MM_TN == 0 for o in outs)
    tiles = [_row_tiles(m, tm, wd, seq_len, tmaj) for wd, _, tmaj in outs]
    return pl.pallas_call(
        _matmul_kernel,
        grid=(m // tm,),
        in_specs=[pl.BlockSpec((tm, k), lambda i: (i, 0)),
                  pl.BlockSpec((k, n), lambda i: (0, 0))],
        out_specs=[spec for _, spec in tiles],
        out_shape=[jax.ShapeDtypeStruct(shape, o[1]) for (shape, _), o in zip(tiles, outs)],
        compiler_params=_params(("parallel",)),
        name="proj_matmul",
    )(x, w)


def _mix_ffn_kernel(*refs, n_in):
    a_refs = refs[:n_in]
    wo_refs = refs[n_in:2 * n_in]
    h_ref, g1_ref, b1_ref, w1_ref, w3_ref, w2_ref, g2_ref, b2_ref, o_ref = refs[2 * n_in:]
    mix = jnp.dot(a_refs[0][...], wo_refs[0][...], preferred_element_type=F32)
    for a_ref, w_ref in zip(a_refs[1:], wo_refs[1:]):
        mix += jnp.dot(a_ref[...], w_ref[...], preferred_element_type=F32)
    x = _layer_norm_rows(DEEPNORM_ALPHA * h_ref[...] + mix, g1_ref[...], b1_ref[...])
    xb = x.astype(BF16)
    acc = DEEPNORM_ALPHA * x
    for f in range(0, w1_ref.shape[1], FFN_TF):
        h1 = jnp.dot(xb, w1_ref[:, f:f + FFN_TF], preferred_element_type=F32)
        h3 = jnp.dot(xb, w3_ref[:, f:f + FFN_TF], preferred_element_type=F32)
        gated = (h1 * jax.nn.sigmoid(h1) * h3).astype(BF16)
        acc += jnp.dot(gated, w2_ref[f:f + FFN_TF, :], preferred_element_type=F32)
    o_ref[...] = _layer_norm_rows(acc, g2_ref[...], b2_ref[...])


def _mix_ffn(acts, w_outs, h, g1, b1, w1, w3, w2, g2, b2, seq_len):
    m, d = h.shape
    dff = w1.shape[1]
    tm = min(FFN_TM, m)
    assert m % tm == 0 and dff % FFN_TF == 0
    n_in = len(acts)
    act_specs = []
    for (a, tmaj), w in zip(acts, w_outs):
        shape, spec = _row_tiles(m, tm, w.shape[0], seq_len, tmaj)
        assert a.shape == shape
        act_specs.append(spec)
    acts = [a for a, _ in acts]
    const = lambda arr: pl.BlockSpec(arr.shape, lambda i: (0, 0), pipeline_mode=pl.Buffered(1))
    row_tile = pl.BlockSpec((tm, d), lambda i: (i, 0))
    return pl.pallas_call(
        functools.partial(_mix_ffn_kernel, n_in=n_in),
        grid=(m // tm,),
        in_specs=(act_specs + [const(w) for w in w_outs]
                  + [row_tile, const(g1), const(b1), const(w1), const(w3), const(w2),
                     const(g2), const(b2)]),
        out_specs=row_tile,
        out_shape=jax.ShapeDtypeStruct((m, d), F32),
        compiler_params=_params(("parallel",)),
        name="mix_ffn",
    )(*acts, *w_outs, h, g1, b1, w1, w3, w2, g2, b2)


def _chunk_attn_kernel(q_ref, k0_ref, k1_ref, k2_ref, v0_ref, v1_ref, v2_ref, bias_ref, o_ref):
    qb = pl.program_id(2)
    q = q_ref[0]
    kk = jnp.concatenate([k0_ref[0], k1_ref[0], k2_ref[0]], axis=0)
    vv = jnp.concatenate([v0_ref[0], v1_ref[0], v2_ref[0]], axis=0)
    lane = lax.broadcasted_iota(jnp.int32, q.shape, 1)
    qs = q * jnp.asarray(ATTN_SCALE, q.dtype)
    col = lax.broadcasted_iota(jnp.int32, (q.shape[0], kk.shape[0]), 1)
    col_valid = col >= (2 - qb) * q.shape[0]
    out = jnp.zeros(q.shape, F32)
    for hh in range(q.shape[1] // HEAD_DIM):
        in_head = (lane >= hh * HEAD_DIM) & (lane < (hh + 1) * HEAD_DIM)
        qm = jnp.where(in_head, qs, jnp.zeros_like(qs))
        s = lax.dot_general(qm, kk, (((1,), (1,)), ((), ())), preferred_element_type=F32)
        s = jnp.where(col_valid, s + bias_ref[hh], NEG_BIG)
        m = jnp.max(s, axis=-1, keepdims=True)
        e = jnp.exp(s - m)
        l = jnp.sum(e, axis=-1, keepdims=True)
        pv = jnp.dot(e.astype(BF16), vv, preferred_element_type=F32)
        out = jnp.where(in_head, pv / l, out)
    o_ref[0] = out.astype(o_ref.dtype)


def _chunk_attention(qkv, bias):
    b_, s_, _ = qkv.shape
    n_grp = MIX_A_WIDTH // CA_LANES
    nq = s_ // CA_QB

    def kv_spec(col0, back):
        return pl.BlockSpec((1, CA_QB, CA_LANES),
                            lambda b, p, i: (b, jnp.maximum(i - back, 0), col0 + p))

    return pl.pallas_call(
        _chunk_attn_kernel,
        grid=(b_, n_grp, nq),
        in_specs=[pl.BlockSpec((1, CA_QB, CA_LANES), lambda b, p, i: (b, i, p)),
                  kv_spec(n_grp, 2), kv_spec(n_grp, 1), kv_spec(n_grp, 0),
                  kv_spec(2 * n_grp, 2), kv_spec(2 * n_grp, 1), kv_spec(2 * n_grp, 0),
                  pl.BlockSpec((CA_LANES // HEAD_DIM, CA_QB, CA_KB), lambda b, p, i: (p, 0, 0))],
        out_specs=pl.BlockSpec((1, CA_QB, CA_LANES), lambda b, p, i: (b, i, p)),
        out_shape=jax.ShapeDtypeStruct((b_, s_, MIX_A_WIDTH), BF16),
        compiler_params=_params(("parallel", "parallel", "arbitrary")),
        name="chunk_attn",
    )(qkv, qkv, qkv, qkv, qkv, qkv, qkv, bias)


def _band_bias(rel_table):
    assert CA_KB == CA_QB + LEFT_CHUNKS * CHUNK
    qi = jnp.arange(CA_QB)
    ki = jnp.arange(CA_KB)
    dist = (qi[:, None] // CHUNK + LEFT_CHUNKS) - (ki[None, :] // CHUNK)
    in_band = (dist >= 0) & (dist <= LEFT_CHUNKS)
    span = CA_QB + CA_KB - 1
    lo_pad = CA_KB - 1 - LEFT_CHUNKS * CHUNK - MAX_REL
    hi_pad = span - lo_pad - (2 * MAX_REL + 1)
    ext = jnp.pad(rel_table.astype(F32), ((0, 0), (lo_pad, hi_pad)), mode='edge')
    r = jnp.pad(ext[:, ::-1], ((0, 0), (0, 1)))
    shifted = jnp.tile(r, (1, CA_QB))[:, :CA_QB * span].reshape(-1, CA_QB, span)
    bias = shifted[:, :, CA_QB - 1:CA_QB - 1 + CA_KB]
    return jnp.where(in_band[None], bias, NEG_BIG)


def _ssm_kernel(u_ref, bblk_ref, ar_ref, ai_ref, cblk_ref, d_ref, gw_ref, gb_ref, o_ref,
                bu_ref, st_ref):
    n_rows, width = u_ref.shape
    n_state = ar_ref.shape[1]
    n_tiles = n_state // LANES
    tiles_per_ch = n_tiles // (width // LANES)

    @pl.when(pl.program_id(0) == 0)
    def _():
        st_ref[...] = jnp.zeros_like(st_ref)

    u = u_ref[...]
    ub = u.astype(BF16)
    for half in range(2):
        for k in range(0, n_tiles, 2):
            ch = (k // tiles_per_ch) * LANES
            col = half * n_state + k * LANES
            res = jnp.dot(ub[:, ch:ch + LANES], bblk_ref[ch:ch + LANES, col:col + 2 * LANES],
                          preferred_element_type=F32)
            bu_ref[half * n_tiles + k] = res[:, :LANES]
            bu_ref[half * n_tiles + k + 1] = res[:, LANES:]

    group = SSM_LC // LANES
    half = SUBLANES // 2
    first = lax.broadcasted_iota(jnp.int32, (SUBLANES, LANES), 0) < half
    for c in range(0, n_tiles, group):
        ar = [jnp.broadcast_to(ar_ref[:, pl.ds((c + k) * LANES, LANES)], (SUBLANES, LANES))
              for k in range(group)]
        ai = [jnp.broadcast_to(ai_ref[:, pl.ds((c + k) * LANES, LANES)], (SUBLANES, LANES))
              for k in range(group)]

        def two_steps(kt, carry, c=c, ar=ar, ai=ai):
            rows = pl.ds(pl.multiple_of(kt * SUBLANES, SUBLANES), SUBLANES)
            out = []
            for k in range(group):
                xr, xi = carry[2 * k], carry[2 * k + 1]
                br = bu_ref[c + k, rows, :]
                bi = bu_ref[n_tiles + c + k, rows, :]
                xr, xi = pltpu.roll(xr, half, 0), pltpu.roll(xi, half, 0)
                er = ar[k] * xr - ai[k] * xi + br
                ei = ar[k] * xi + ai[k] * xr + bi
                xr, xi = pltpu.roll(er, half, 0), pltpu.roll(ei, half, 0)
                orr = ar[k] * xr - ai[k] * xi + br
                oi = ar[k] * xi + ai[k] * xr + bi
                bu_ref[c + k, rows, :] = jnp.where(first, er, orr)
                bu_ref[n_tiles + c + k, rows, :] = jnp.where(first, ei, oi)
                out += [orr, oi]
            return tuple(out)

        init = []
        for k in range(group):
            init += [st_ref[c + k], st_ref[n_tiles + c + k]]
        fin = lax.fori_loop(0, n_rows // SUBLANES, two_steps, tuple(init), unroll=4)
        for k in range(group):
            st_ref[c + k] = fin[2 * k]
            st_ref[n_tiles + c + k] = fin[2 * k + 1]

    ys = []
    for m in range(width // LANES):
        t0 = m * tiles_per_ch
        xs = [bu_ref[half * n_tiles + t0 + k] for half in range(2) for k in range(tiles_per_ch)]
        rows = [pl.ds(half * n_state + t0 * LANES, tiles_per_ch * LANES) for half in range(2)]
        cs = jnp.concatenate([cblk_ref[rows[0], m * LANES:(m + 1) * LANES],
                              cblk_ref[rows[1], m * LANES:(m + 1) * LANES]], axis=0)
        ys.append(jnp.dot(jnp.concatenate(xs, axis=1).astype(BF16), cs,
                          preferred_element_type=F32))
    y = jnp.concatenate(ys, axis=1)
    y = jax.nn.gelu(y + d_ref[...] * u)
    gate = jax.nn.sigmoid(jnp.dot(y.astype(BF16), gw_ref[...], preferred_element_type=F32)
                          + gb_ref[...])
    o_ref[...] = (y * gate).astype(o_ref.dtype)


def _ssm_glu(u_tm, b_, bblk, ar, ai, cblk, d_skip, glu_w, glu_b):
    width = u_tm.shape[1]
    s_ = u_tm.shape[0] // b_
    n_state = ar.shape[1]
    tt = min(SSM_T, s_)
    assert s_ % tt == 0 and n_state % SSM_LC == 0
    assert 2 * b_ == SUBLANES, "the scan packs two time steps of a batch of 4 into one 8-row tile"
    rows = tt * b_
    const = lambda shape: pl.BlockSpec(shape, lambda t: (0,) * len(shape))
    return pl.pallas_call(
        _ssm_kernel,
        grid=(s_ // tt,),
        in_specs=[pl.BlockSpec((rows, width), lambda t: (t, 0)),
                  const(bblk.shape), const(ar.shape), const(ai.shape), const(cblk.shape),
                  const(d_skip.shape), const(glu_w.shape), const(glu_b.shape)],
        out_specs=pl.BlockSpec((rows, width), lambda t: (t, 0)),
        out_shape=jax.ShapeDtypeStruct((s_ * b_, width), BF16),
        scratch_shapes=[pltpu.VMEM((2 * n_state // LANES, rows, LANES), F32),
                        pltpu.VMEM((2 * n_state // LANES, SUBLANES, LANES), F32)],
        compiler_params=_params(("arbitrary",)),
        name="ssm_glu",
    )(u_tm, bblk, ar, ai, cblk, d_skip, glu_w, glu_b)


def _ssm_tables(lam_re, lam_im, b_re, b_im, c_re, c_im, log_dt):
    dt = jnp.exp(log_dt.astype(F32))[:, None]
    lr, li = lam_re.astype(F32), lam_im.astype(F32)
    mag = jnp.exp(lr * dt)
    ar = mag * jnp.cos(li * dt)
    ai = mag * jnp.sin(li * dt)
    den = lr * lr + li * li
    fr = ((ar - 1.0) * lr + ai * li) / den
    fi = (ai * lr - (ar - 1.0) * li) / den
    br_, bi_ = b_re.astype(F32), b_im.astype(F32)
    bbar_r = fr[..., None] * br_ - fi[..., None] * bi_
    bbar_i = fr[..., None] * bi_ + fi[..., None] * br_
    eye = jnp.eye(SSM_GROUPS, dtype=F32)
    n_state = SSM_GROUPS * SSM_STATE

    def in_map(bb):
        return jnp.einsum('gpi,gh->gihp', bb, eye).reshape(MIX_B_WIDTH, n_state)

    def out_map(cc):
        return jnp.einsum('gip,gh->gphi', cc, eye).reshape(n_state, MIX_B_WIDTH)

    bblk = jnp.concatenate([in_map(bbar_r), in_map(bbar_i)], axis=1).astype(BF16)
    cblk = jnp.concatenate([out_map(c_re.astype(F32)), -out_map(c_im.astype(F32))],
                           axis=0).astype(BF16)
    return bblk, ar.reshape(1, n_state), ai.reshape(1, n_state), cblk


def _log_sigmoids(z):
    tail = jnp.log(1.0 + jnp.exp2(-jnp.abs(z * LOG2E)))
    return jnp.minimum(z, 0.0) - tail, jnp.maximum(z, 0.0) + tail


def _sb_attn_kernel(q_ref, k_ref, v_ref, tri_ref, lt_ref, o_ref, qst_ref, acc_ref, carry_ref):
    i = pl.program_id(2)
    lane = lax.broadcasted_iota(jnp.int32, (SB_KB, LANES), 1)
    for sub in range(2):
        qs = q_ref[0, sub * SB_KB:(sub + 1) * SB_KB, :]
        qs = qs * jnp.asarray(ATTN_SCALE, qs.dtype)
        zero = jnp.zeros_like(qs)
        qst_ref[(2 * sub) * SB_KB:(2 * sub + 1) * SB_KB, :] = jnp.where(lane < HEAD_DIM, qs, zero)
        qst_ref[(2 * sub + 1) * SB_KB:(2 * sub + 2) * SB_KB, :] = jnp.where(lane >= HEAD_DIM, qs,
                                                                            zero)
    acc_ref[...] = jnp.zeros_like(acc_ref)
    carry_ref[...] = jnp.zeros_like(carry_ref)
    n_rows = 4 * SB_KB

    def causal(tile):
        lt2 = jnp.concatenate([lt_ref[...], lt_ref[...]], axis=0)
        if tile.shape[0] == lt2.shape[0]:
            return tile * lt2
        return jnp.concatenate([tile[:lt2.shape[0]] * lt2, tile[lt2.shape[0]:]], axis=0)

    def key_block(j, r0, r1, diagonal, gate=None):
        start = pl.multiple_of(j * SB_KB, SB_KB)
        kblk = k_ref[0, pl.ds(start, SB_KB), :]
        vblk = v_ref[0, pl.ds(start, SB_KB), :]
        z = lax.dot_general(qst_ref[r0:r1, :], kblk, (((1,), (1,)), ((), ())),
                            preferred_element_type=F32)
        log_beta, sp = _log_sigmoids(z)
        if diagonal:
            sp = causal(sp)
        c = jnp.dot(sp.astype(BF16), tri_ref[...], preferred_element_type=F32)
        w = jnp.exp(log_beta + c)
        if diagonal:
            w = causal(w)
        pv = jnp.dot(w.astype(BF16), vblk, preferred_element_type=F32)
        total = jnp.broadcast_to(c[:, 0:1] - sp[:, 0:1], pv.shape)
        if gate is not None:
            pv, total = pv * gate, total * gate
        carry = carry_ref[r0:r1, :]
        acc_ref[r0:r1, :] += jnp.exp(carry) * pv
        carry_ref[r0:r1, :] = carry + total

    key_block(2 * i + 1, 2 * SB_KB, n_rows, True)
    key_block(2 * i, 0, n_rows, True)
    key_block(jnp.maximum(2 * i - 1, 0), 0, 2 * SB_KB, False, gate=(i > 0).astype(F32))

    def alive(r0, r1):
        return jnp.max(carry_ref[r0:r1, :]) >= EXP_ZERO_BELOW

    @pl.when((i > 0) & alive(2 * SB_KB, n_rows))
    def _():
        key_block(2 * i - 1, 2 * SB_KB, n_rows, False)

    def sweep(first_j, r0, r1, watch0, watch1):
        def cond(state):
            j, still_alive = state
            return (j >= 0) & still_alive

        def body(state):
            j, _ = state
            key_block(j, r0, r1, False)
            return j - 1, alive(watch0, watch1)

        return lax.while_loop(cond, body, (first_j, alive(watch0, watch1)))[0]

    j_next = sweep(2 * i - 2, 0, n_rows, 2 * SB_KB, n_rows)
    sweep(j_next, 0, 2 * SB_KB, 0, 2 * SB_KB)

    for sub in range(2):
        a0 = acc_ref[(2 * sub) * SB_KB:(2 * sub + 1) * SB_KB, :]
        a1 = acc_ref[(2 * sub + 1) * SB_KB:(2 * sub + 2) * SB_KB, :]
        o_ref[0, sub * SB_KB:(sub + 1) * SB_KB, :] = jnp.where(lane < HEAD_DIM, a0,
                                                               a1).astype(o_ref.dtype)


def _stick_breaking_attention(qkv):
    b_, s_, width = qkv.shape
    n_pairs = width // 3 // LANES
    assert SB_QB == 2 * SB_KB and s_ % SB_QB == 0
    ridx = jnp.arange(SB_KB)
    tri = jnp.where(ridx[:, None] > ridx[None, :], -1.0, 0.0).astype(BF16)
    lt = jnp.where(ridx[None, :] < ridx[:, None], 1.0, 0.0).astype(F32)
    return pl.pallas_call(
        _sb_attn_kernel,
        grid=(b_, n_pairs, s_ // SB_QB),
        in_specs=[pl.BlockSpec((1, SB_QB, LANES), lambda b, p, i: (b, i, p)),
                  pl.BlockSpec((1, s_, LANES), lambda b, p, i: (b, 0, n_pairs + p)),
                  pl.BlockSpec((1, s_, LANES), lambda b, p, i: (b, 0, 2 * n_pairs + p)),
                  pl.BlockSpec((SB_KB, SB_KB), lambda b, p, i: (0, 0)),
                  pl.BlockSpec((SB_KB, SB_KB), lambda b, p, i: (0, 0))],
        out_specs=pl.BlockSpec((1, SB_QB, LANES), lambda b, p, i: (b, i, p)),
        out_shape=jax.ShapeDtypeStruct((b_, s_, width // 3), BF16),
        scratch_shapes=[pltpu.VMEM((2 * SB_QB, LANES), BF16), pltpu.VMEM((2 * SB_QB, LANES), F32),
                        pltpu.VMEM((2 * SB_QB, LANES), F32)],
        compiler_params=_params(("parallel", "parallel", "arbitrary")),
        name="sb_attn",
    )(qkv, qkv, qkv, tri, lt)


def kernel(x, even_w_in, even_rel_bias, ssm_lambda_re, ssm_lambda_im, ssm_b_re, ssm_b_im,
           ssm_c_re, ssm_c_im, ssm_d, ssm_log_dt, ssm_glu_w, ssm_glu_b, even_w_out,
           odd_w_in, odd_w_out, ffn_w1, ffn_w3, ffn_w2, ln_mix_g, ln_mix_b, ln_ffn_g, ln_ffn_b):
    b_, s_, d = x.shape
    m = b_ * s_
    row = lambda v: v.astype(F32).reshape(1, -1)
    h = x.reshape(m, d).astype(F32)
    for layer in range(DEPTH):
        if layer % 2 == 0:
            e = layer // 2
            qkv, u_tm = _matmul(h, even_w_in[e].astype(BF16),
                                [(3 * MIX_A_WIDTH, BF16, False), (MIX_B_WIDTH, F32, True)], s_)
            qkv = qkv.reshape(b_, s_, 3 * MIX_A_WIDTH)
            oa = _chunk_attention(qkv, _band_bias(even_rel_bias[e]))
            bblk, ar, ai, cblk = _ssm_tables(ssm_lambda_re[e], ssm_lambda_im[e], ssm_b_re[e],
                                             ssm_b_im[e], ssm_c_re[e], ssm_c_im[e], ssm_log_dt[e])
            ob_tm = _ssm_glu(u_tm.reshape(s_ * b_, MIX_B_WIDTH), b_, bblk, ar, ai, cblk,
                             row(ssm_d[e]), ssm_glu_w[e].astype(BF16), row(ssm_glu_b[e]))
            w_out = even_w_out[e].astype(BF16)
            acts = [(oa.reshape(m, MIX_A_WIDTH), False),
                    (ob_tm.reshape(s_, b_ * MIX_B_WIDTH), True)]
            weights = [w_out[:MIX_A_WIDTH], w_out[MIX_A_WIDTH:]]
        else:
            o = layer // 2
            (qkv,) = _matmul(h, odd_w_in[o].astype(BF16), [(3 * d, BF16, False)], s_)
            qkv = qkv.reshape(b_, s_, 3 * d)
            oc = _stick_breaking_attention(qkv)
            acts = [(oc.reshape(m, d), False)]
            weights = [odd_w_out[o].astype(BF16)]
        h = _mix_ffn(acts, weights, h, row(ln_mix_g[layer]), row(ln_mix_b[layer]),
                     ffn_w1[layer].astype(BF16), ffn_w3[layer].astype(BF16),
                     ffn_w2[layer].astype(BF16), row(ln_ffn_g[layer]), row(ln_ffn_b[layer]), s_)
    return h.reshape(b_, s_, d).astype(x.dtype)
```

```python
import functools
import math

import jax
import jax.numpy as jnp
from jax import lax
from jax.experimental import pallas as pl
from jax.experimental.pallas import tpu as pltpu

D_MODEL = 1024
DEPTH = 2
CHUNK = 64
LEFT_CHUNKS = 8
HEAD_DIM = 64
MIX_A_WIDTH = 512
MIX_B_WIDTH = 512
A_HEADS = 8
MAX_REL = 128
SSM_GROUP = 16
SSM_GROUPS = 32
SSM_STATE = 64
C_HEADS = 16
DEEPNORM_ALPHA = (2 * DEPTH) ** 0.25
LN_EPS = 1e-5
ATTN_SCALE = HEAD_DIM ** -0.5
NEG_BIG = -1e30
LOG2E = 1.0 / math.log(2.0)
EXP_ZERO_BELOW = -104.0

LANES = 128
SUBLANES = 8
VMEM_LIMIT_BYTES = 56 * 1024 * 1024

MM_TM, MM_TN = 1024, 512
FFN_TM, FFN_TF = 512, 256
CA_QB = 256
CA_KB = 3 * CA_QB
CA_LANES = 256
SSM_T = 128
SSM_LC = 512
SB_QB = 512
SB_KB = 256

BF16 = jnp.bfloat16
F32 = jnp.float32


def _params(semantics):
    return pltpu.CompilerParams(dimension_semantics=semantics, vmem_limit_bytes=VMEM_LIMIT_BYTES)


def _layer_norm_rows(y, g, b):
    mu = jnp.mean(y, axis=-1, keepdims=True)
    yc = y - mu
    var = jnp.mean(yc * yc, axis=-1, keepdims=True)
    return yc * lax.rsqrt(var + LN_EPS) * g + b


def _matmul_kernel(x_ref, w_ref, *o_refs):
    xb = x_ref[...].astype(BF16)
    col = 0
    for o_ref in o_refs:
        width = o_ref.shape[1]
        for c in range(0, width, MM_TN):
            o_ref[:, c:c + MM_TN] = jnp.dot(xb, w_ref[:, col + c:col + c + MM_TN],
                                            preferred_element_type=F32).astype(o_ref.dtype)
        col += width


def _row_tiles(m, tm, width, seq_len, time_major):
    if not time_major:
        return (m, width), pl.BlockSpec((tm, width), lambda i: (i, 0))
    assert seq_len % tm == 0
    tiles_per_seq = seq_len // tm
    return ((seq_len, (m // seq_len) * width),
            pl.BlockSpec((tm, width), lambda i: (i % tiles_per_seq, i // tiles_per_seq)))


def _matmul(x, w, outs, seq_len):
    m, k = x.shape
    n = w.shape[1]
    tm = min(MM_TM, m)
    assert m % tm == 0 and sum(o[0] for o in outs) == n and all(o[0] % MM_TN == 0 for o in outs)
    tiles = [_row_tiles(m, tm, wd, seq_len, tmaj) for wd, _, tmaj in outs]
    return pl.pallas_call(
        _matmul_kernel,
        grid=(m // tm,),
        in_specs=[pl.BlockSpec((tm, k), lambda i: (i, 0)),
                  pl.BlockSpec((k, n), lambda i: (0, 0))],
        out_specs=[spec for _, spec in tiles],
        out_shape=[jax.ShapeDtypeStruct(shape, o[1]) for (shape, _), o in zip(tiles, outs)],
        compiler_params=_params(("parallel",)),
        name="proj_matmul",
    )(x, w)


def _mix_ffn_kernel(*refs, n_in):
    a_refs = refs[:n_in]
    wo_refs = refs[n_in:2 * n_in]
    h_ref, g1_ref, b1_ref, w1_ref, w3_ref, w2_ref, g2_ref, b2_ref, o_ref = refs[2 * n_in:]
    mix = jnp.dot(a_refs[0][...], wo_refs[0][...], preferred_element_type=F32)
    for a_ref, w_ref in zip(a_refs[1:], wo_refs[1:]):
        mix += jnp.dot(a_ref[...], w_ref[...], preferred_element_type=F32)
    x = _layer_norm_rows(DEEPNORM_ALPHA * h_ref[...] + mix, g1_ref[...], b1_ref[...])
    xb = x.astype(BF16)
    acc = DEEPNORM_ALPHA * x
    for f in range(0, w1_ref.shape[1], FFN_TF):
        h1 = jnp.dot(xb, w1_ref[:, f:f + FFN_TF], preferred_element_type=F32)
        h3 = jnp.dot(xb, w3_ref[:, f:f + FFN_TF], preferred_element_type=F32)
        gated = (h1 * jax.nn.sigmoid(h1) * h3).astype(BF16)
        acc += jnp.dot(gated, w2_ref[f:f + FFN_TF, :], preferred_element_type=F32)
    o_ref[...] = _layer_norm_rows(acc, g2_ref[...], b2_ref[...])


def _mix_ffn(acts, w_outs, h, g1, b1, w1, w3, w2, g2, b2, seq_len):
    m, d = h.shape
    dff = w1.shape[1]
    tm = min(FFN_TM, m)
    assert m % tm == 0 and dff % FFN_TF == 0
    n_in = len(acts)
    act_specs = []
    for (a, tmaj), w in zip(acts, w_outs):
        shape, spec = _row_tiles(m, tm, w.shape[0], seq_len, tmaj)
        assert a.shape == shape
        act_specs.append(spec)
    acts = [a for a, _ in acts]
    const = lambda arr: pl.BlockSpec(arr.shape, lambda i: (0, 0), pipeline_mode=pl.Buffered(1))
    row_tile = pl.BlockSpec((tm, d), lambda i: (i, 0))
    return pl.pallas_call(
        functools.partial(_mix_ffn_kernel, n_in=n_in),
        grid=(m // tm,),
        in_specs=(act_specs + [const(w) for w in w_outs]
                  + [row_tile, const(g1), const(b1), const(w1), const(w3), const(w2),
                     const(g2), const(b2)]),
        out_specs=row_tile,
        out_shape=jax.ShapeDtypeStruct((m, d), F32),
        compiler_params=_params(("parallel",)),
        name="mix_ffn",
    )(*acts, *w_outs, h, g1, b1, w1, w3, w2, g2, b2)


def _chunk_attn_kernel(q_ref, k0_ref, k1_ref, k2_ref, v0_ref, v1_ref, v2_ref, bias_ref, o_ref):
    qb = pl.program_id(2)
    q = q_ref[0]
    kk = jnp.concatenate([k0_ref[0], k1_ref[0], k2_ref[0]], axis=0)
    vv = jnp.concatenate([v0_ref[0], v1_ref[0], v2_ref[0]], axis=0)
    lane = lax.broadcasted_iota(jnp.int32, q.shape, 1)
    qs = q * jnp.asarray(ATTN_SCALE, q.dtype)
    col = lax.broadcasted_iota(jnp.int32, (q.shape[0], kk.shape[0]), 1)
    col_valid = col >= (2 - qb) * q.shape[0]
    out = jnp.zeros(q.shape, F32)
    for hh in range(q.shape[1] // HEAD_DIM):
        in_head = (lane >= hh * HEAD_DIM) & (lane < (hh + 1) * HEAD_DIM)
        qm = jnp.where(in_head, qs, jnp.zeros_like(qs))
        s = lax.dot_general(qm, kk, (((1,), (1,)), ((), ())), preferred_element_type=F32)
        s = jnp.where(col_valid, s + bias_ref[hh], NEG_BIG)
        m = jnp.max(s, axis=-1, keepdims=True)
        e = jnp.exp(s - m)
        l = jnp.sum(e, axis=-1, keepdims=True)
        pv = jnp.dot(e.astype(BF16), vv, preferred_element_type=F32)
        out = jnp.where(in_head, pv / l, out)
    o_ref[0] = out.astype(o_ref.dtype)


def _chunk_attention(qkv, bias):
    b_, s_, _ = qkv.shape
    n_grp = MIX_A_WIDTH // CA_LANES
    nq = s_ // CA_QB

    def kv_spec(col0, back):
        return pl.BlockSpec((1, CA_QB, CA_LANES),
                            lambda b, p, i: (b, jnp.maximum(i - back, 0), col0 + p))

    return pl.pallas_call(
        _chunk_attn_kernel,
        grid=(b_, n_grp, nq),
        in_specs=[pl.BlockSpec((1, CA_QB, CA_LANES), lambda b, p, i: (b, i, p)),
                  kv_spec(n_grp, 2), kv_spec(n_grp, 1), kv_spec(n_grp, 0),
                  kv_spec(2 * n_grp, 2), kv_spec(2 * n_grp, 1), kv_spec(2 * n_grp, 0),
                  pl.BlockSpec((CA_LANES // HEAD_DIM, CA_QB, CA_KB), lambda b, p, i: (p, 0, 0))],
        out_specs=pl.BlockSpec((1, CA_QB, CA_LANES), lambda b, p, i: (b, i, p)),
        out_shape=jax.ShapeDtypeStruct((b_, s_, MIX_A_WIDTH), BF16),
        compiler_params=_params(("parallel", "parallel", "arbitrary")),
        name="chunk_attn",
    )(qkv, qkv, qkv, qkv, qkv, qkv, qkv, bias)


def _band_bias(rel_table):
    assert CA_KB == CA_QB + LEFT_CHUNKS * CHUNK
    qi = jnp.arange(CA_QB)
    ki = jnp.arange(CA_KB)
    dist = (qi[:, None] // CHUNK + LEFT_CHUNKS) - (ki[None, :] // CHUNK)
    in_band = (dist >= 0) & (dist <= LEFT_CHUNKS)
    span = CA_QB + CA_KB - 1
    lo_pad = CA_KB - 1 - LEFT_CHUNKS * CHUNK - MAX_REL
    hi_pad = span - lo_pad - (2 * MAX_REL + 1)
    ext = jnp.pad(rel_table.astype(F32), ((0, 0), (lo_pad, hi_pad)), mode='edge')
    r = jnp.pad(ext[:, ::-1], ((0, 0), (0, 1)))
    shifted = jnp.tile(r, (1, CA_QB))[:, :CA_QB * span].reshape(-1, CA_QB, span)
    bias = shifted[:, :, CA_QB - 1:CA_QB - 1 + CA_KB]
    return jnp.where(in_band[None], bias, NEG_BIG)


def _ssm_kernel(u_ref, bblk_ref, ar_ref, ai_ref, cblk_ref, d_ref, gw_ref, gb_ref, o_ref,
                bu_ref, st_ref):
    n_rows, width = u_ref.shape
    n_state = ar_ref.shape[1]
    n_tiles = n_state // LANES
    tiles_per_ch = n_tiles // (width // LANES)

    @pl.when(pl.program_id(0) == 0)
    def _():
        st_ref[...] = jnp.zeros_like(st_ref)

    u = u_ref[...]
    ub = u.astype(BF16)
    for half in range(2):
        for k in range(0, n_tiles, 2):
            ch = (k // tiles_per_ch) * LANES
            col = half * n_state + k * LANES
            res = jnp.dot(ub[:, ch:ch + LANES], bblk_ref[ch:ch + LANES, col:col + 2 * LANES],
                          preferred_element_type=F32)
            bu_ref[half * n_tiles + k] = res[:, :LANES]
            bu_ref[half * n_tiles + k + 1] = res[:, LANES:]

    group = SSM_LC // LANES
    half = SUBLANES // 2
    first = lax.broadcasted_iota(jnp.int32, (SUBLANES, LANES), 0) < half
    for c in range(0, n_tiles, group):
        ar = [jnp.broadcast_to(ar_ref[:, pl.ds((c + k) * LANES, LANES)], (SUBLANES, LANES))
              for k in range(group)]
        ai = [jnp.broadcast_to(ai_ref[:, pl.ds((c + k) * LANES, LANES)], (SUBLANES, LANES))
              for k in range(group)]

        def two_steps(kt, carry, c=c, ar=ar, ai=ai):
            rows = pl.ds(pl.multiple_of(kt * SUBLANES, SUBLANES), SUBLANES)
            out = []
            for k in range(group):
                xr, xi = carry[2 * k], carry[2 * k + 1]
                br = bu_ref[c + k, rows, :]
                bi = bu_ref[n_tiles + c + k, rows, :]
                xr, xi = pltpu.roll(xr, half, 0), pltpu.roll(xi, half, 0)
                er = ar[k] * xr - ai[k] * xi + br
                ei = ar[k] * xi + ai[k] * xr + bi
                xr, xi = pltpu.roll(er, half, 0), pltpu.roll(ei, half, 0)
                orr = ar[k] * xr - ai[k] * xi + br
                oi = ar[k] * xi + ai[k] * xr + bi
                bu_ref[c + k, rows, :] = jnp.where(first, er, orr)
                bu_ref[n_tiles + c + k, rows, :] = jnp.where(first, ei, oi)
                out += [orr, oi]
            return tuple(out)

        init = []
        for k in range(group):
            init += [st_ref[c + k], st_ref[n_tiles + c + k]]
        fin = lax.fori_loop(0, n_rows // SUBLANES, two_steps, tuple(init), unroll=4)
        for k in range(group):
            st_ref[c + k] = fin[2 * k]
            st_ref[n_tiles + c + k] = fin[2 * k + 1]

    ys = []
    for m in range(width // LANES):
        t0 = m * tiles_per_ch
        xs = [bu_ref[half * n_tiles + t0 + k] for half in range(2) for k in range(tiles_per_ch)]
        rows = [pl.ds(half * n_state + t0 * LANES, tiles_per_ch * LANES) for half in range(2)]
        cs = jnp.concatenate([cblk_ref[rows[0], m * LANES:(m + 1) * LANES],
                              cblk_ref[rows[1], m * LANES:(m + 1) * LANES]], axis=0)
        ys.append(jnp.dot(jnp.concatenate(xs, axis=1).astype(BF16), cs,
                          preferred_element_type=F32))
    y = jnp.concatenate(ys, axis=1)
    y = jax.nn.gelu(y + d_ref[...] * u)
    gate = jax.nn.sigmoid(jnp.dot(y.astype(BF16), gw_ref[...], preferred_element_type=F32)
                          + gb_ref[...])
    o_ref[...] = (y * gate).astype(o_ref.dtype)


def _ssm_glu(u_tm, b_, bblk, ar, ai, cblk, d_skip, glu_w, glu_b):
    width = u_tm.shape[1]
    s_ = u_tm.shape[0] // b_
    n_state = ar.shape[1]
    tt = min(SSM_T, s_)
    assert s_ % tt == 0 and n_state % SSM_LC == 0
    assert 2 * b_ == SUBLANES, "the scan packs two time steps of a batch of 4 into one 8-row tile"
    rows = tt * b_
    const = lambda shape: pl.BlockSpec(shape, lambda t: (0,) * len(shape))
    return pl.pallas_call(
        _ssm_kernel,
        grid=(s_ // tt,),
        in_specs=[pl.BlockSpec((rows, width), lambda t: (t, 0)),
                  const(bblk.shape), const(ar.shape), const(ai.shape), const(cblk.shape),
                  const(d_skip.shape), const(glu_w.shape), const(glu_b.shape)],
        out_specs=pl.BlockSpec((rows, width), lambda t: (t, 0)),
        out_shape=jax.ShapeDtypeStruct((s_ * b_, width), BF16),
        scratch_shapes=[pltpu.VMEM((2 * n_state // LANES, rows, LANES), F32),
                        pltpu.VMEM((2 * n_state // LANES, SUBLANES, LANES), F32)],
        compiler_params=_params(("arbitrary",)),
        name="ssm_glu",
    )(u_tm, bblk, ar, ai, cblk, d_skip, glu_w, glu_b)


def _ssm_tables(lam_re, lam_im, b_re, b_im, c_re, c_im, log_dt):
    dt = jnp.exp(log_dt.astype(F32))[:, None]
    lr, li = lam_re.astype(F32), lam_im.astype(F32)
    mag = jnp.exp(lr * dt)
    ar = mag * jnp.cos(li * dt)
    ai = mag * jnp.sin(li * dt)
    den = lr * lr + li * li
    fr = ((ar - 1.0) * lr + ai * li) / den
    fi = (ai * lr - (ar - 1.0) * li) / den
    br_, bi_ = b_re.astype(F32), b_im.astype(F32)
    bbar_r = fr[..., None] * br_ - fi[..., None] * bi_
    bbar_i = fr[..., None] * bi_ + fi[..., None] * br_
    eye = jnp.eye(SSM_GROUPS, dtype=F32)
    n_state = SSM_GROUPS * SSM_STATE

    def in_map(bb):
        return jnp.einsum('gpi,gh->gihp', bb, eye).reshape(MIX_B_WIDTH, n_state)

    def out_map(cc):
        return jnp.einsum('gip,gh->gphi', cc, eye).reshape(n_state, MIX_B_WIDTH)

    bblk = jnp.concatenate([in_map(bbar_r), in_map(bbar_i)], axis=1).astype(BF16)
    cblk = jnp.concatenate([out_map(c_re.astype(F32)), -out_map(c_im.astype(F32))],
                           axis=0).astype(BF16)
    return bblk, ar.reshape(1, n_state), ai.reshape(1, n_state), cblk


def _log_sigmoids(z):
    sp = jnp.maximum(z, 0.0) + jnp.log(1.0 + jnp.exp2(-jnp.abs(z * LOG2E)))
    return z - sp, sp


def _sb_attn_kernel(q_ref, k_ref, v_ref, tri_ref, lt_ref, o_ref, qst_ref, acc_ref, carry_ref):
    i = pl.program_id(2)
    lane = lax.broadcasted_iota(jnp.int32, (SB_KB, LANES), 1)
    for sub in range(2):
        qs = q_ref[0, sub * SB_KB:(sub + 1) * SB_KB, :]
        qs = qs * jnp.asarray(ATTN_SCALE, qs.dtype)
        zero = jnp.zeros_like(qs)
        qst_ref[(2 * sub) * SB_KB:(2 * sub + 1) * SB_KB, :] = jnp.where(lane < HEAD_DIM, qs, zero)
        qst_ref[(2 * sub + 1) * SB_KB:(2 * sub + 2) * SB_KB, :] = jnp.where(lane >= HEAD_DIM, qs,
                                                                            zero)
    acc_ref[...] = jnp.zeros_like(acc_ref)
    carry_ref[...] = jnp.zeros_like(carry_ref)
    n_rows = 4 * SB_KB

    def causal(tile):
        lt2 = jnp.concatenate([lt_ref[...], lt_ref[...]], axis=0)
        if tile.shape[0] == lt2.shape[0]:
            return tile * lt2
        return jnp.concatenate([tile[:lt2.shape[0]] * lt2, tile[lt2.shape[0]:]], axis=0)

    def key_block(j, r0, r1, diagonal, gate=None):
        start = pl.multiple_of(j * SB_KB, SB_KB)
        kblk = k_ref[0, pl.ds(start, SB_KB), :]
        vblk = v_ref[0, pl.ds(start, SB_KB), :]
        z = lax.dot_general(qst_ref[r0:r1, :], kblk, (((1,), (1,)), ((), ())),
                            preferred_element_type=F32)
        log_beta, sp = _log_sigmoids(z)
        if diagonal:
            sp = causal(sp)
        c = jnp.dot(sp.astype(BF16), tri_ref[...], preferred_element_type=F32)
        w = jnp.exp(log_beta + c)
        if diagonal:
            w = causal(w)
        pv = jnp.dot(w.astype(BF16), vblk, preferred_element_type=F32)
        total = jnp.broadcast_to(c[:, 0:1] - sp[:, 0:1], pv.shape)
        if gate is not None:
            pv, total = pv * gate, total * gate
        carry = carry_ref[r0:r1, :]
        acc_ref[r0:r1, :] += jnp.exp(carry) * pv
        carry_ref[r0:r1, :] = carry + total

    key_block(2 * i + 1, 2 * SB_KB, n_rows, True)
    key_block(2 * i, 0, n_rows, True)
    key_block(jnp.maximum(2 * i - 1, 0), 0, 2 * SB_KB, False, gate=(i > 0).astype(F32))

    def alive(r0, r1):
        return jnp.max(carry_ref[r0:r1, :]) >= EXP_ZERO_BELOW

    @pl.when((i > 0) & alive(2 * SB_KB, n_rows))
    def _():
        key_block(2 * i - 1, 2 * SB_KB, n_rows, False)

    def sweep(first_j, r0, r1, watch0, watch1):
        def cond(state):
            j, still_alive = state
            return (j >= 0) & still_alive

        def body(state):
            j, _ = state
            key_block(j, r0, r1, False)
            return j - 1, alive(watch0, watch1)

        return lax.while_loop(cond, body, (first_j, alive(watch0, watch1)))[0]

    j_next = sweep(2 * i - 2, 0, n_rows, 2 * SB_KB, n_rows)
    sweep(j_next, 0, 2 * SB_KB, 0, 2 * SB_KB)

    for sub in range(2):
        a0 = acc_ref[(2 * sub) * SB_KB:(2 * sub + 1) * SB_KB, :]
        a1 = acc_ref[(2 * sub + 1) * SB_KB:(2 * sub + 2) * SB_KB, :]
        o_ref[0, sub * SB_KB:(sub + 1) * SB_KB, :] = jnp.where(lane < HEAD_DIM, a0,
                                                               a1).astype(o_ref.dtype)


def _stick_breaking_attention(qkv):
    b_, s_, width = qkv.shape
    n_pairs = width // 3 // LANES
    assert SB_QB == 2 * SB_KB and s_ % SB_QB == 0
    ridx = jnp.arange(SB_KB)
    tri = jnp.where(ridx[:, None] > ridx[None, :], -1.0, 0.0).astype(BF16)
    lt = jnp.where(ridx[None, :] < ridx[:, None], 1.0, 0.0).astype(F32)
    return pl.pallas_call(
        _sb_attn_kernel,
        grid=(b_, n_pairs, s_ // SB_QB),
        in_specs=[pl.BlockSpec((1, SB_QB, LANES), lambda b, p, i: (b, i, p)),
                  pl.BlockSpec((1, s_, LANES), lambda b, p, i: (b, 0, n_pairs + p)),
                  pl.BlockSpec((1, s_, LANES), lambda b, p, i: (b, 0, 2 * n_pairs + p)),
                  pl.BlockSpec((SB_KB, SB_KB), lambda b, p, i: (0, 0)),
                  pl.BlockSpec((SB_KB, SB_KB), lambda b, p, i: (0, 0))],
        out_specs=pl.BlockSpec((1, SB_QB, LANES), lambda b, p, i: (b, i, p)),
        out_shape=jax.ShapeDtypeStruct((b_, s_, width // 3), BF16),
        scratch_shapes=[pltpu.VMEM((2 * SB_QB, LANES), BF16), pltpu.VMEM((2 * SB_QB, LANES), F32),
                        pltpu.VMEM((2 * SB_QB, LANES), F32)],
        compiler_params=_params(("parallel", "parallel", "arbitrary")),
        name="sb_attn",
    )(qkv, qkv, qkv, tri, lt)


def kernel(x, even_w_in, even_rel_bias, ssm_lambda_re, ssm_lambda_im, ssm_b_re, ssm_b_im,
           ssm_c_re, ssm_c_im, ssm_d, ssm_log_dt, ssm_glu_w, ssm_glu_b, even_w_out,
           odd_w_in, odd_w_out, ffn_w1, ffn_w3, ffn_w2, ln_mix_g, ln_mix_b, ln_ffn_g, ln_ffn_b):
    b_, s_, d = x.shape
    m = b_ * s_
    row = lambda v: v.astype(F32).reshape(1, -1)
    h = x.reshape(m, d).astype(F32)
    for layer in range(DEPTH):
        if layer % 2 == 0:
            e = layer // 2
            qkv, u_tm = _matmul(h, even_w_in[e].astype(BF16),
                                [(3 * MIX_A_WIDTH, BF16, False), (MIX_B_WIDTH, F32, True)], s_)
            qkv = qkv.reshape(b_, s_, 3 * MIX_A_WIDTH)
            oa = _chunk_attention(qkv, _band_bias(even_rel_bias[e]))
            bblk, ar, ai, cblk = _ssm_tables(ssm_lambda_re[e], ssm_lambda_im[e], ssm_b_re[e],
                                             ssm_b_im[e], ssm_c_re[e], ssm_c_im[e], ssm_log_dt[e])
            ob_tm = _ssm_glu(u_tm.reshape(s_ * b_, MIX_B_WIDTH), b_, bblk, ar, ai, cblk,
                             row(ssm_d[e]), ssm_glu_w[e].astype(BF16), row(ssm_glu_b[e]))
            w_out = even_w_out[e].astype(BF16)
            acts = [(oa.reshape(m, MIX_A_WIDTH), False),
                    (ob_tm.reshape(s_, b_ * MIX_B_WIDTH), True)]
            weights = [w_out[:MIX_A_WIDTH], w_out[MIX_A_WIDTH:]]
        else:
            o = layer // 2
            (qkv,) = _matmul(h, odd_w_in[o].astype(BF16), [(3 * d, BF16, False)], s_)
            qkv = qkv.reshape(b_, s_, 3 * d)
            oc = _stick_breaking_attention(qkv)
            acts = [(oc.reshape(m, d), False)]
            weights = [odd_w_out[o].astype(BF16)]
        h = _mix_ffn(acts, weights, h, row(ln_mix_g[layer]), row(ln_mix_b[layer]),
                     ffn_w1[layer].astype(BF16), ffn_w3[layer].astype(BF16),
                     ffn_w2[layer].astype(BF16), row(ln_ffn_g[layer]), row(ln_ffn_b[layer]), s_)
    return h.reshape(b_, s_, d).astype(x.dtype)
```

```python
import functools
import math

import jax
import jax.numpy as jnp
from jax import lax
from jax.experimental import pallas as pl
from jax.experimental.pallas import tpu as pltpu

D_MODEL = 1024
DEPTH = 2
CHUNK = 64
LEFT_CHUNKS = 8
HEAD_DIM = 64
MIX_A_WIDTH = 512
MIX_B_WIDTH = 512
A_HEADS = 8
MAX_REL = 128
SSM_GROUP = 16
SSM_GROUPS = 32
SSM_STATE = 64
C_HEADS = 16
DEEPNORM_ALPHA = (2 * DEPTH) ** 0.25
LN_EPS = 1e-5
ATTN_SCALE = HEAD_DIM ** -0.5
NEG_BIG = -1e30
LOG2E = 1.0 / math.log(2.0)
EXP_ZERO_BELOW = -104.0

LANES = 128
SUBLANES = 8
VMEM_LIMIT_BYTES = 56 * 1024 * 1024

MM_TM, MM_TN = 1024, 512
FFN_TM, FFN_TF = 512, 256
CA_QB = 256
CA_KB = 3 * CA_QB
CA_LANES = 256
SSM_T = 128
SSM_LC = 512
SB_QB = 512
SB_KB = 256

BF16 = jnp.bfloat16
F32 = jnp.float32


def _params(semantics):
    return pltpu.CompilerParams(dimension_semantics=semantics, vmem_limit_bytes=VMEM_LIMIT_BYTES)


def _layer_norm_rows(y, g, b):
    mu = jnp.mean(y, axis=-1, keepdims=True)
    yc = y - mu
    var = jnp.mean(yc * yc, axis=-1, keepdims=True)
    return yc * lax.rsqrt(var + LN_EPS) * g + b


def _matmul_kernel(x_ref, w_ref, *o_refs):
    xb = x_ref[...].astype(BF16)
    col = 0
    for o_ref in o_refs:
        width = o_ref.shape[1]
        for c in range(0, width, MM_TN):
            o_ref[:, c:c + MM_TN] = jnp.dot(xb, w_ref[:, col + c:col + c + MM_TN],
                                            preferred_element_type=F32).astype(o_ref.dtype)
        col += width


def _row_tiles(m, tm, width, seq_len, time_major):
    if not time_major:
        return (m, width), pl.BlockSpec((tm, width), lambda i: (i, 0))
    assert seq_len % tm == 0
    tiles_per_seq = seq_len // tm
    return ((seq_len, (m // seq_len) * width),
            pl.BlockSpec((tm, width), lambda i: (i % tiles_per_seq, i // tiles_per_seq)))


def _matmul(x, w, outs, seq_len):
    m, k = x.shape
    n = w.shape[1]
    tm = min(MM_TM, m)
    assert m % tm == 0 and sum(o[0] for o in outs) == n and all(o[0] % MM_TN == 0 for o in outs)
    tiles = [_row_tiles(m, tm, wd, seq_len, tmaj) for wd, _, tmaj in outs]
    return pl.pallas_call(
        _matmul_kernel,
        grid=(m // tm,),
        in_specs=[pl.BlockSpec((tm, k), lambda i: (i, 0)),
                  pl.BlockSpec((k, n), lambda i: (0, 0))],
        out_specs=[spec for _, spec in tiles],
        out_shape=[jax.ShapeDtypeStruct(shape, o[1]) for (shape, _), o in zip(tiles, outs)],
        compiler_params=_params(("parallel",)),
        name="proj_matmul",
    )(x, w)


def _mix_ffn_kernel(*refs, n_in):
    a_refs = refs[:n_in]
    wo_refs = refs[n_in:2 * n_in]
    h_ref, g1_ref, b1_ref, w1_ref, w3_ref, w2_ref, g2_ref, b2_ref, o_ref = refs[2 * n_in:]
    mix = jnp.dot(a_refs[0][...], wo_refs[0][...], preferred_element_type=F32)
    for a_ref, w_ref in zip(a_refs[1:], wo_refs[1:]):
        mix += jnp.dot(a_ref[...], w_ref[...], preferred_element_type=F32)
    x = _layer_norm_rows(DEEPNORM_ALPHA * h_ref[...] + mix, g1_ref[...], b1_ref[...])
    xb = x.astype(BF16)
    acc = DEEPNORM_ALPHA * x
    for f in range(0, w1_ref.shape[1], FFN_TF):
        h1 = jnp.dot(xb, w1_ref[:, f:f + FFN_TF], preferred_element_type=F32)
        h3 = jnp.dot(xb, w3_ref[:, f:f + FFN_TF], preferred_element_type=F32)
        gated = (h1 * jax.nn.sigmoid(h1) * h3).astype(BF16)
        acc += jnp.dot(gated, w2_ref[f:f + FFN_TF, :], preferred_element_type=F32)
    o_ref[...] = _layer_norm_rows(acc, g2_ref[...], b2_ref[...])


def _mix_ffn(acts, w_outs, h, g1, b1, w1, w3, w2, g2, b2, seq_len):
    m, d = h.shape
    dff = w1.shape[1]
    tm = min(FFN_TM, m)
    assert m % tm == 0 and dff % FFN_TF == 0
    n_in = len(acts)
    act_specs = []
    for (a, tmaj), w in zip(acts, w_outs):
        shape, spec = _row_tiles(m, tm, w.shape[0], seq_len, tmaj)
        assert a.shape == shape
        act_specs.append(spec)
    acts = [a for a, _ in acts]
    const = lambda arr: pl.BlockSpec(arr.shape, lambda i: (0, 0), pipeline_mode=pl.Buffered(1))
    row_tile = pl.BlockSpec((tm, d), lambda i: (i, 0))
    return pl.pallas_call(
        functools.partial(_mix_ffn_kernel, n_in=n_in),
        grid=(m // tm,),
        in_specs=(act_specs + [const(w) for w in w_outs]
                  + [row_tile, const(g1), const(b1), const(w1), const(w3), const(w2),
                     const(g2), const(b2)]),
        out_specs=row_tile,
        out_shape=jax.ShapeDtypeStruct((m, d), F32),
        compiler_params=_params(("parallel",)),
        name="mix_ffn",
    )(*acts, *w_outs, h, g1, b1, w1, w3, w2, g2, b2)


def _chunk_attn_kernel(q_ref, k0_ref, k1_ref, k2_ref, v0_ref, v1_ref, v2_ref, bias_ref, o_ref):
    qb = pl.program_id(2)
    q = q_ref[0]
    kk = jnp.concatenate([k0_ref[0], k1_ref[0], k2_ref[0]], axis=0)
    vv = jnp.concatenate([v0_ref[0], v1_ref[0], v2_ref[0]], axis=0)
    lane = lax.broadcasted_iota(jnp.int32, q.shape, 1)
    qs = q * jnp.asarray(ATTN_SCALE, q.dtype)
    col = lax.broadcasted_iota(jnp.int32, (q.shape[0], kk.shape[0]), 1)
    col_valid = col >= (2 - qb) * q.shape[0]
    out = jnp.zeros(q.shape, F32)
    for hh in range(q.shape[1] // HEAD_DIM):
        in_head = (lane >= hh * HEAD_DIM) & (lane < (hh + 1) * HEAD_DIM)
        qm = jnp.where(in_head, qs, jnp.zeros_like(qs))
        s = lax.dot_general(qm, kk, (((1,), (1,)), ((), ())), preferred_element_type=F32)
        s = jnp.where(col_valid, s + bias_ref[hh], NEG_BIG)
        m = jnp.max(s, axis=-1, keepdims=True)
        e = jnp.exp(s - m)
        l = jnp.sum(e, axis=-1, keepdims=True)
        pv = jnp.dot(e.astype(BF16), vv, preferred_element_type=F32)
        out = jnp.where(in_head, pv / l, out)
    o_ref[0] = out.astype(o_ref.dtype)


def _chunk_attention(qkv, bias):
    b_, s_, _ = qkv.shape
    n_grp = MIX_A_WIDTH // CA_LANES
    nq = s_ // CA_QB

    def kv_spec(col0, back):
        return pl.BlockSpec((1, CA_QB, CA_LANES),
                            lambda b, p, i: (b, jnp.maximum(i - back, 0), col0 + p))

    return pl.pallas_call(
        _chunk_attn_kernel,
        grid=(b_, n_grp, nq),
        in_specs=[pl.BlockSpec((1, CA_QB, CA_LANES), lambda b, p, i: (b, i, p)),
                  kv_spec(n_grp, 2), kv_spec(n_grp, 1), kv_spec(n_grp, 0),
                  kv_spec(2 * n_grp, 2), kv_spec(2 * n_grp, 1), kv_spec(2 * n_grp, 0),
                  pl.BlockSpec((CA_LANES // HEAD_DIM, CA_QB, CA_KB), lambda b, p, i: (p, 0, 0))],
        out_specs=pl.BlockSpec((1, CA_QB, CA_LANES), lambda b, p, i: (b, i, p)),
        out_shape=jax.ShapeDtypeStruct((b_, s_, MIX_A_WIDTH), BF16),
        compiler_params=_params(("parallel", "parallel", "arbitrary")),
        name="chunk_attn",
    )(qkv, qkv, qkv, qkv, qkv, qkv, qkv, bias)


def _band_bias(rel_table):
    assert CA_KB == CA_QB + LEFT_CHUNKS * CHUNK
    qi = jnp.arange(CA_QB)
    ki = jnp.arange(CA_KB)
    dist = (qi[:, None] // CHUNK + LEFT_CHUNKS) - (ki[None, :] // CHUNK)
    in_band = (dist >= 0) & (dist <= LEFT_CHUNKS)
    span = CA_QB + CA_KB - 1
    lo_pad = CA_KB - 1 - LEFT_CHUNKS * CHUNK - MAX_REL
    hi_pad = span - lo_pad - (2 * MAX_REL + 1)
    ext = jnp.pad(rel_table.astype(F32), ((0, 0), (lo_pad, hi_pad)), mode='edge')
    r = jnp.pad(ext[:, ::-1], ((0, 0), (0, 1)))
    shifted = jnp.tile(r, (1, CA_QB))[:, :CA_QB * span].reshape(-1, CA_QB, span)
    bias = shifted[:, :, CA_QB - 1:CA_QB - 1 + CA_KB]
    return jnp.where(in_band[None], bias, NEG_BIG)


def _ssm_kernel(u_ref, bblk_ref, ar_ref, ai_ref, cblk_ref, d_ref, gw_ref, gb_ref, o_ref,
                bu_ref, st_ref):
    n_rows, width = u_ref.shape
    n_state = ar_ref.shape[1]
    n_tiles = n_state // LANES
    tiles_per_ch = n_tiles // (width // LANES)

    @pl.when(pl.program_id(0) == 0)
    def _():
        st_ref[...] = jnp.zeros_like(st_ref)

    u = u_ref[...]
    ub = u.astype(BF16)
    for half in range(2):
        for k in range(0, n_tiles, 2):
            ch = (k // tiles_per_ch) * LANES
            col = half * n_state + k * LANES
            res = jnp.dot(ub[:, ch:ch + LANES], bblk_ref[ch:ch + LANES, col:col + 2 * LANES],
                          preferred_element_type=F32)
            bu_ref[half * n_tiles + k] = res[:, :LANES]
            bu_ref[half * n_tiles + k + 1] = res[:, LANES:]

    group = SSM_LC // LANES
    half = SUBLANES // 2
    first = lax.broadcasted_iota(jnp.int32, (SUBLANES, LANES), 0) < half
    for c in range(0, n_tiles, group):
        ar = [jnp.broadcast_to(ar_ref[:, pl.ds((c + k) * LANES, LANES)], (SUBLANES, LANES))
              for k in range(group)]
        ai = [jnp.broadcast_to(ai_ref[:, pl.ds((c + k) * LANES, LANES)], (SUBLANES, LANES))
              for k in range(group)]

        def two_steps(kt, carry, c=c, ar=ar, ai=ai):
            rows = pl.ds(pl.multiple_of(kt * SUBLANES, SUBLANES), SUBLANES)
            out = []
            for k in range(group):
                xr, xi = carry[2 * k], carry[2 * k + 1]
                br = bu_ref[c + k, rows, :]
                bi = bu_ref[n_tiles + c + k, rows, :]
                xr, xi = pltpu.roll(xr, half, 0), pltpu.roll(xi, half, 0)
                er = ar[k] * xr - ai[k] * xi + br
                ei = ar[k] * xi + ai[k] * xr + bi
                xr, xi = pltpu.roll(er, half, 0), pltpu.roll(ei, half, 0)
                orr = ar[k] * xr - ai[k] * xi + br
                oi = ar[k] * xi + ai[k] * xr + bi
                bu_ref[c + k, rows, :] = jnp.where(first, er, orr)
                bu_ref[n_tiles + c + k, rows, :] = jnp.where(first, ei, oi)
                out += [orr, oi]
            return tuple(out)

        init = []
        for k in range(group):
            init += [st_ref[c + k], st_ref[n_tiles + c + k]]
        fin = lax.fori_loop(0, n_rows // SUBLANES, two_steps, tuple(init), unroll=4)
        for k in range(group):
            st_ref[c + k] = fin[2 * k]
            st_ref[n_tiles + c + k] = fin[2 * k + 1]

    ys = []
    for m in range(width // LANES):
        t0 = m * tiles_per_ch
        xs = [bu_ref[half * n_tiles + t0 + k] for half in range(2) for k in range(tiles_per_ch)]
        rows = [pl.ds(half * n_state + t0 * LANES, tiles_per_ch * LANES) for half in range(2)]
        cs = jnp.concatenate([cblk_ref[rows[0], m * LANES:(m + 1) * LANES],
                              cblk_ref[rows[1], m * LANES:(m + 1) * LANES]], axis=0)
        ys.append(jnp.dot(jnp.concatenate(xs, axis=1).astype(BF16), cs,
                          preferred_element_type=F32))
    y = jnp.concatenate(ys, axis=1)
    y = jax.nn.gelu(y + d_ref[...] * u)
    gate = jax.nn.sigmoid(jnp.dot(y.astype(BF16), gw_ref[...], preferred_element_type=F32)
                          + gb_ref[...])
    o_ref[...] = (y * gate).astype(o_ref.dtype)


def _ssm_glu(u_tm, b_, bblk, ar, ai, cblk, d_skip, glu_w, glu_b):
    width = u_tm.shape[1]
    s_ = u_tm.shape[0] // b_
    n_state = ar.shape[1]
    tt = min(SSM_T, s_)
    assert s_ % tt == 0 and n_state % SSM_LC == 0
    assert 2 * b_ == SUBLANES, "the scan packs two time steps of a batch of 4 into one 8-row tile"
    rows = tt * b_
    const = lambda shape: pl.BlockSpec(shape, lambda t: (0,) * len(shape))
    return pl.pallas_call(
        _ssm_kernel,
        grid=(s_ // tt,),
        in_specs=[pl.BlockSpec((rows, width), lambda t: (t, 0)),
                  const(bblk.shape), const(ar.shape), const(ai.shape), const(cblk.shape),
                  const(d_skip.shape), const(glu_w.shape), const(glu_b.shape)],
        out_specs=pl.BlockSpec((rows, width), lambda t: (t, 0)),
        out_shape=jax.ShapeDtypeStruct((s_ * b_, width), BF16),
        scratch_shapes=[pltpu.VMEM((2 * n_state // LANES, rows, LANES), F32),
                        pltpu.VMEM((2 * n_state // LANES, SUBLANES, LANES), F32)],
        compiler_params=_params(("arbitrary",)),
        name="ssm_glu",
    )(u_tm, bblk, ar, ai, cblk, d_skip, glu_w, glu_b)


def _ssm_tables(lam_re, lam_im, b_re, b_im, c_re, c_im, log_dt):
    dt = jnp.exp(log_dt.astype(F32))[:, None]
    lr, li = lam_re.astype(F32), lam_im.astype(F32)
    mag = jnp.exp(lr * dt)
    ar = mag * jnp.cos(li * dt)
    ai = mag * jnp.sin(li * dt)
    den = lr * lr + li * li
    fr = ((ar - 1.0) * lr + ai * li) / den
    fi = (ai * lr - (ar - 1.0) * li) / den
    br_, bi_ = b_re.astype(F32), b_im.astype(F32)
    bbar_r = fr[..., None] * br_ - fi[..., None] * bi_
    bbar_i = fr[..., None] * bi_ + fi[..., None] * br_
    eye = jnp.eye(SSM_GROUPS, dtype=F32)
    n_state = SSM_GROUPS * SSM_STATE

    def in_map(bb):
        return jnp.einsum('gpi,gh->gihp', bb, eye).reshape(MIX_B_WIDTH, n_state)

    def out_map(cc):
        return jnp.einsum('gip,gh->gphi', cc, eye).reshape(n_state, MIX_B_WIDTH)

    bblk = jnp.concatenate([in_map(bbar_r), in_map(bbar_i)], axis=1).astype(BF16)
    cblk = jnp.concatenate([out_map(c_re.astype(F32)), -out_map(c_im.astype(F32))],
                           axis=0).astype(BF16)
    return bblk, ar.reshape(1, n_state), ai.reshape(1, n_state), cblk


def _log_sigmoids(z):
    sp = jnp.maximum(z, 0.0) + jnp.log(1.0 + jnp.exp2(-jnp.abs(z * LOG2E)))
    return z - sp, sp


def _sb_attn_kernel(q_ref, k_ref, v_ref, tri_ref, lt_ref, o_ref, qst_ref, acc_ref, carry_ref):
    i = pl.program_id(2)
    lane = lax.broadcasted_iota(jnp.int32, (SB_KB, LANES), 1)
    for sub in range(2):
        qs = q_ref[0, sub * SB_KB:(sub + 1) * SB_KB, :]
        qs = qs * jnp.asarray(ATTN_SCALE, qs.dtype)
        zero = jnp.zeros_like(qs)
        qst_ref[(2 * sub) * SB_KB:(2 * sub + 1) * SB_KB, :] = jnp.where(lane < HEAD_DIM, qs, zero)
        qst_ref[(2 * sub + 1) * SB_KB:(2 * sub + 2) * SB_KB, :] = jnp.where(lane >= HEAD_DIM, qs,
                                                                            zero)
    acc_ref[...] = jnp.zeros_like(acc_ref)
    carry_ref[...] = jnp.zeros_like(carry_ref)
    n_rows = 4 * SB_KB

    def causal(tile):
        lt2 = jnp.concatenate([lt_ref[...], lt_ref[...]], axis=0)
        if tile.shape[0] == lt2.shape[0]:
            return tile * lt2
        return jnp.concatenate([tile[:lt2.shape[0]] * lt2, tile[lt2.shape[0]:]], axis=0)

    def key_block(j, r0, r1, diagonal, gate=None):
        start = pl.multiple_of(j * SB_KB, SB_KB)
        kblk = k_ref[0, pl.ds(start, SB_KB), :]
        vblk = v_ref[0, pl.ds(start, SB_KB), :]
        z = lax.dot_general(qst_ref[r0:r1, :], kblk, (((1,), (1,)), ((), ())),
                            preferred_element_type=F32)
        log_beta, sp = _log_sigmoids(z)
        if diagonal:
            sp = causal(sp)
        c = jnp.dot(sp.astype(BF16), tri_ref[...], preferred_element_type=F32)
        w = jnp.exp(log_beta + c)
        if diagonal:
            w = causal(w)
        pv = jnp.dot(w.astype(BF16), vblk, preferred_element_type=F32)
        total = jnp.broadcast_to(c[:, 0:1] - sp[:, 0:1], pv.shape)
        if gate is not None:
            pv, total = pv * gate, total * gate
        carry = carry_ref[r0:r1, :]
        acc_ref[r0:r1, :] += jnp.exp(carry) * pv
        carry_ref[r0:r1, :] = carry + total

    key_block(2 * i + 1, 2 * SB_KB, n_rows, True)
    key_block(2 * i, 0, n_rows, True)
    key_block(jnp.maximum(2 * i - 1, 0), 0, 2 * SB_KB, False, gate=(i > 0).astype(F32))

    def alive(r0, r1):
        return jnp.max(carry_ref[r0:r1, :]) >= EXP_ZERO_BELOW

    def sweep(first_j, r0, r1, watch0, watch1):
        def cond(state):
            j, still_alive = state
            return (j >= 0) & still_alive

        def body(state):
            j, _ = state
            key_block(j, r0, r1, False)
            return j - 1, alive(watch0, watch1)

        return lax.while_loop(cond, body, (first_j, alive(watch0, watch1)))[0]

    @pl.when((i > 0) & alive(0, n_rows))
    def _():
        @pl.when(alive(2 * SB_KB, n_rows))
        def _():
            key_block(2 * i - 1, 2 * SB_KB, n_rows, False)

        j_next = sweep(2 * i - 2, 0, n_rows, 2 * SB_KB, n_rows)
        sweep(j_next, 0, 2 * SB_KB, 0, 2 * SB_KB)

    for sub in range(2):
        a0 = acc_ref[(2 * sub) * SB_KB:(2 * sub + 1) * SB_KB, :]
        a1 = acc_ref[(2 * sub + 1) * SB_KB:(2 * sub + 2) * SB_KB, :]
        o_ref[0, sub * SB_KB:(sub + 1) * SB_KB, :] = jnp.where(lane < HEAD_DIM, a0,
                                                               a1).astype(o_ref.dtype)


def _stick_breaking_attention(qkv):
    b_, s_, width = qkv.shape
    n_pairs = width // 3 // LANES
    assert SB_QB == 2 * SB_KB and s_ % SB_QB == 0
    ridx = jnp.arange(SB_KB)
    tri = jnp.where(ridx[:, None] > ridx[None, :], -1.0, 0.0).astype(BF16)
    lt = jnp.where(ridx[None, :] < ridx[:, None], 1.0, 0.0).astype(F32)
    return pl.pallas_call(
        _sb_attn_kernel,
        grid=(b_, n_pairs, s_ // SB_QB),
        in_specs=[pl.BlockSpec((1, SB_QB, LANES), lambda b, p, i: (b, i, p)),
                  pl.BlockSpec((1, s_, LANES), lambda b, p, i: (b, 0, n_pairs + p)),
                  pl.BlockSpec((1, s_, LANES), lambda b, p, i: (b, 0, 2 * n_pairs + p)),
                  pl.BlockSpec((SB_KB, SB_KB), lambda b, p, i: (0, 0)),
                  pl.BlockSpec((SB_KB, SB_KB), lambda b, p, i: (0, 0))],
        out_specs=pl.BlockSpec((1, SB_QB, LANES), lambda b, p, i: (b, i, p)),
        out_shape=jax.ShapeDtypeStruct((b_, s_, width // 3), BF16),
        scratch_shapes=[pltpu.VMEM((2 * SB_QB, LANES), BF16), pltpu.VMEM((2 * SB_QB, LANES), F32),
                        pltpu.VMEM((2 * SB_QB, LANES), F32)],
        compiler_params=_params(("parallel", "parallel", "arbitrary")),
        name="sb_attn",
    )(qkv, qkv, qkv, tri, lt)


def kernel(x, even_w_in, even_rel_bias, ssm_lambda_re, ssm_lambda_im, ssm_b_re, ssm_b_im,
           ssm_c_re, ssm_c_im, ssm_d, ssm_log_dt, ssm_glu_w, ssm_glu_b, even_w_out,
           odd_w_in, odd_w_out, ffn_w1, ffn_w3, ffn_w2, ln_mix_g, ln_mix_b, ln_ffn_g, ln_ffn_b):
    b_, s_, d = x.shape
    m = b_ * s_
    row = lambda v: v.astype(F32).reshape(1, -1)
    h = x.reshape(m, d).astype(F32)
    for layer in range(DEPTH):
        if layer % 2 == 0:
            e = layer // 2
            qkv, u_tm = _matmul(h, even_w_in[e].astype(BF16),
                                [(3 * MIX_A_WIDTH, BF16, False), (MIX_B_WIDTH, F32, True)], s_)
            qkv = qkv.reshape(b_, s_, 3 * MIX_A_WIDTH)
            oa = _chunk_attention(qkv, _band_bias(even_rel_bias[e]))
            bblk, ar, ai, cblk = _ssm_tables(ssm_lambda_re[e], ssm_lambda_im[e], ssm_b_re[e],
                                             ssm_b_im[e], ssm_c_re[e], ssm_c_im[e], ssm_log_dt[e])
            ob_tm = _ssm_glu(u_tm.reshape(s_ * b_, MIX_B_WIDTH), b_, bblk, ar, ai, cblk,
                             row(ssm_d[e]), ssm_glu_w[e].astype(BF16), row(ssm_glu_b[e]))
            w_out = even_w_out[e].astype(BF16)
            acts = [(oa.reshape(m, MIX_A_WIDTH), False),
                    (ob_tm.reshape(s_, b_ * MIX_B_WIDTH), True)]
            weights = [w_out[:MIX_A_WIDTH], w_out[MIX_A_WIDTH:]]
        else:
            o = layer // 2
            (qkv,) = _matmul(h, odd_w_in[o].astype(BF16), [(3 * d, BF16, False)], s_)
            qkv = qkv.reshape(b_, s_, 3 * d)
            oc = _stick_breaking_attention(qkv)
            acts = [(oc.reshape(m, d), False)]
            weights = [odd_w_out[o].astype(BF16)]
        h = _mix_ffn(acts, weights, h, row(ln_mix_g[layer]), row(ln_mix_b[layer]),
                     ffn_w1[layer].astype(BF16), ffn_w3[layer].astype(BF16),
                     ffn_w2[layer].astype(BF16), row(ln_ffn_g[layer]), row(ln_ffn_b[layer]), s_)
    return h.reshape(b_, s_, d).astype(x.dtype)
```
